```python
import math
import jax, jax.numpy as jnp
from jax import lax
import numpy as np

D_MODEL = 1024
BATCH = 4
SEQ = 8192
DEPTH = 1
DEC_BATCH = 128
DEC_SEQ = 4
PAST_LEN = 8192
PAGE_SIZE = 128

H_A = 4
HD = 64
DK = 2 * HD
W_A = H_A * DK
H_B = 4
CB = 128
W_B = H_B * CB
CHUNK = 128
D_MIX = W_A + W_B
D_IN = 4 * W_A + 3 * W_B
N_BUCKETS = 32
MAX_DISTANCE = 128
Q_BLOCK = 128
LN_EPS = 1e-5
ALPHA = (2.0 * DEPTH) ** 0.25
BETA = (8.0 * DEPTH) ** -0.25
SCALE = HD ** -0.5
NEG = -1e30

kernel_name = 'hybrid_diffattn_sgu_decoder_step'


def _layer_norm(x, g, b):
    xf = x.astype(jnp.float32)
    mu = jnp.mean(xf, axis=-1, keepdims=True)
    var = jnp.mean(jnp.square(xf - mu), axis=-1, keepdims=True)
    y = (xf - mu) * lax.rsqrt(var + LN_EPS) * g.astype(jnp.float32) + b.astype(jnp.float32)
    return y.astype(x.dtype)


def _rms_norm(x, g):
    xf = x.astype(jnp.float32)
    y = xf * lax.rsqrt(jnp.mean(jnp.square(xf), axis=-1, keepdims=True) + LN_EPS) * g.astype(jnp.float32)
    return y.astype(x.dtype)


def _t5_bucket(dist):
    n = jnp.maximum(dist, 0)
    max_exact = N_BUCKETS // 2
    nf = jnp.maximum(n, 1).astype(jnp.float32)
    large = max_exact + (jnp.log(nf / max_exact) / math.log(MAX_DISTANCE / max_exact)
                         * (N_BUCKETS - max_exact)).astype(jnp.int32)
    large = jnp.minimum(large, N_BUCKETS - 1)
    return jnp.where(n < max_exact, n, large)


def _rel_bias(dist, table):
    return jnp.transpose(table[_t5_bucket(dist)].astype(jnp.float32), (2, 0, 1))


def _diff_probs(l1, l2, mask, lam):
    p1 = jax.nn.softmax(jnp.where(mask, l1, NEG), axis=-1)
    p2 = jax.nn.softmax(jnp.where(mask, l2, NEG), axis=-1)
    return p1 - lam * p2


def _project(x, w_in):
    h = jnp.einsum('bsd,de->bse', x, w_in)
    cuts = [W_A, 2 * W_A, 3 * W_A, 4 * W_A, 4 * W_A + W_B, 4 * W_A + 2 * W_B]
    q, k, v, ga, u, vb, gb = jnp.split(h, cuts, axis=-1)
    B, S, _ = x.shape
    heads = lambda t: t.reshape(B, S, H_A, DK)
    return heads(q), heads(k), heads(v), ga, u, vb, gb


def _prompt_diff_attn(q, k, v, table, lam):
    B, S, H, _ = q.shape
    nb = S // Q_BLOCK
    q_blocks = jnp.transpose(q.reshape(B, nb, Q_BLOCK, H, DK), (1, 0, 2, 3, 4))
    k1, k2 = k[..., :HD], k[..., HD:]
    k_pos = jnp.arange(S, dtype=jnp.int32)

    def block(args):
        i, qb = args
        q_pos = i * Q_BLOCK + jnp.arange(Q_BLOCK, dtype=jnp.int32)
        dist = q_pos[:, None] - k_pos[None, :]
        mask = dist >= 0
        bias = _rel_bias(dist, table)
        l1 = jnp.einsum('bqhd,bkhd->bhqk', qb[..., :HD], k1).astype(jnp.float32) * SCALE + bias
        l2 = jnp.einsum('bqhd,bkhd->bhqk', qb[..., HD:], k2).astype(jnp.float32) * SCALE + bias
        p = _diff_probs(l1, l2, mask, lam)
        return jnp.einsum('bhqk,bkhe->bqhe', p.astype(v.dtype), v)

    o = lax.map(block, (jnp.arange(nb, dtype=jnp.int32), q_blocks))
    return jnp.transpose(o, (1, 0, 2, 3, 4)).reshape(B, S, H, DK)


def _sample_diff_attn(q, k_new, v_new, k_past, v_past, table, lam):
    P = k_past.shape[1]
    T = q.shape[1]
    q_pos = P + jnp.arange(T, dtype=jnp.int32)
    k_pos = jnp.arange(P + T, dtype=jnp.int32)
    dist = q_pos[:, None] - k_pos[None, :]
    mask = dist >= 0
    bias = _rel_bias(dist, table)
    q1, q2 = q[..., :HD], q[..., HD:]
    l1 = jnp.concatenate([jnp.einsum('bthd,bphd->bhtp', q1, k_past[..., :HD]),
                          jnp.einsum('bthd,bjhd->bhtj', q1, k_new[..., :HD])], axis=-1)
    l2 = jnp.concatenate([jnp.einsum('bthd,bphd->bhtp', q2, k_past[..., HD:]),
                          jnp.einsum('bthd,bjhd->bhtj', q2, k_new[..., HD:])], axis=-1)
    l1 = l1.astype(jnp.float32) * SCALE + bias
    l2 = l2.astype(jnp.float32) * SCALE + bias
    p = _diff_probs(l1, l2, mask, lam).astype(v_new.dtype)
    return (jnp.einsum('bhtp,bphe->bthe', p[..., :P], v_past)
            + jnp.einsum('bhtj,bjhe->bthe', p[..., P:], v_new))


def _attn_finish(o, ga, subln_g, lam_init):
    B, S = o.shape[0], o.shape[1]
    o = _rms_norm(o, subln_g) * (1.0 - lam_init)
    return o.reshape(B, S, W_A) * jax.nn.silu(ga)


def _sgu(u, vb, gb, ln_g, ln_b, w_s, b_s, clen):
    B, S, _ = u.shape
    n = S // clen
    vn = _layer_norm(vb, ln_g, ln_b).reshape(B, S, H_B, CB)
    w = w_s[:, :clen, :clen] * jnp.tril(jnp.ones((clen, clen), w_s.dtype))
    s = jnp.einsum('hts,bcshe->bcthe', w, vn.reshape(B, n, clen, H_B, CB))
    s = s + jnp.transpose(b_s[:, :clen])[None, None, :, :, None]
    y = u * s.reshape(B, S, W_B) * jax.nn.silu(gb)
    return y, vn


def setup_inputs(seed: int = 0) -> dict:
    key = jax.random.key(seed)
    ks = jax.random.split(key, 20)
    n_pages = PAST_LEN // PAGE_SIZE
    n_used = DEC_BATCH * n_pages
    n_pool = (n_used * 5) // 4
    nrm = lambda k, s: jax.random.normal(k, s, jnp.float32)
    page_table = jax.random.permutation(ks[4], n_pool)[:n_used].reshape(DEC_BATCH, n_pages).astype(jnp.int32)
    return {
        'x_prompt': nrm(ks[0], (BATCH, SEQ, D_MODEL)),
        'x_sample': nrm(ks[1], (DEC_BATCH, DEC_SEQ, D_MODEL)),
        'cache_k': nrm(ks[2], (DEPTH, n_pool, PAGE_SIZE, H_A, DK)),
        'cache_v': nrm(ks[3], (DEPTH, n_pool, PAGE_SIZE, H_A, DK)),
        'page_table': page_table,
        'w_in': nrm(ks[5], (DEPTH, D_MODEL, D_IN)) * D_MODEL ** -0.5,
        'w_out': nrm(ks[6], (DEPTH, D_MIX, D_MODEL)) * (D_MIX ** -0.5 * BETA),
        'lambda_q1': nrm(ks[7], (DEPTH, HD)) * 0.1,
        'lambda_k1': nrm(ks[8], (DEPTH, HD)) * 0.1,
        'lambda_q2': nrm(ks[9], (DEPTH, HD)) * 0.1,
        'lambda_k2': nrm(ks[10], (DEPTH, HD)) * 0.1,
        'subln_g': 1.0 + 0.05 * nrm(ks[11], (DEPTH, DK)),
        'rel_bias': nrm(ks[12], (N_BUCKETS, H_A)) * 0.5,
        'sgu_ln_g': 1.0 + 0.05 * nrm(ks[13], (DEPTH, W_B)),
        'sgu_ln_b': 0.05 * nrm(ks[14], (DEPTH, W_B)),
        'sgu_w': nrm(ks[15], (DEPTH, H_B, CHUNK, CHUNK)) * (0.5 * CHUNK ** -0.5),
        'sgu_b': 1.0 + 0.1 * nrm(ks[16], (DEPTH, H_B, CHUNK)),
        'post_ln_g': 1.0 + 0.05 * nrm(ks[17], (DEPTH, D_MODEL)),
        'post_ln_b': 0.05 * nrm(ks[18], (DEPTH, D_MODEL)),
    }


def reference(x_prompt, x_sample, cache_k, cache_v, page_table, w_in, w_out,
              lambda_q1, lambda_k1, lambda_q2, lambda_k2, subln_g, rel_bias,
              sgu_ln_g, sgu_ln_b, sgu_w, sgu_b, post_ln_g, post_ln_b):
    db, n_pages = page_table.shape
    page = cache_k.shape[2]
    past = n_pages * page
    t_s = x_sample.shape[1]
    xp, xs = x_prompt, x_sample
    nkp, nvp, nks, nvs, nsv = [], [], [], [], []
    for l in range(DEPTH):
        lam_init = 0.8 - 0.6 * math.exp(-0.3 * l)
        lam = (jnp.exp(jnp.sum(lambda_q1[l].astype(jnp.float32) * lambda_k1[l].astype(jnp.float32)))
               - jnp.exp(jnp.sum(lambda_q2[l].astype(jnp.float32) * lambda_k2[l].astype(jnp.float32)))
               + lam_init)
        q, k, v, ga, u, vb, gb = _project(xp, w_in[l])
        a = _attn_finish(_prompt_diff_attn(q, k, v, rel_bias, lam), ga, subln_g[l], lam_init)
        g, _ = _sgu(u, vb, gb, sgu_ln_g[l], sgu_ln_b[l], sgu_w[l], sgu_b[l], CHUNK)
        out = jnp.einsum('bse,ed->bsd', jnp.concatenate([a, g], axis=-1), w_out[l])
        xp = _layer_norm(ALPHA * xp + out, post_ln_g[l], post_ln_b[l])
        nkp.append(k)
        nvp.append(v)
        q, k, v, ga, u, vb, gb = _project(xs, w_in[l])
        k_past = cache_k[l][page_table].reshape(db, past, H_A, DK)
        v_past = cache_v[l][page_table].reshape(db, past, H_A, DK)
        a = _attn_finish(_sample_diff_attn(q, k, v, k_past, v_past, rel_bias, lam), ga, subln_g[l], lam_init)
        g, vn = _sgu(u, vb, gb, sgu_ln_g[l], sgu_ln_b[l], sgu_w[l], sgu_b[l], t_s)
        out = jnp.einsum('bse,ed->bsd', jnp.concatenate([a, g], axis=-1), w_out[l])
        xs = _layer_norm(ALPHA * xs + out, post_ln_g[l], post_ln_b[l])
        nks.append(k)
        nvs.append(v)
        nsv.append(vn)
    return (xp, xs, jnp.stack(nkp), jnp.stack(nvp), jnp.stack(nks), jnp.stack(nvs), jnp.stack(nsv))
```

```python
import functools
import math

import numpy as np
import jax
import jax.numpy as jnp
from jax import lax
from jax.experimental import pallas as pl
from jax.experimental.pallas import tpu as pltpu

H_A = 4
HD = 64
DK = 2 * HD
W_A = H_A * DK
H_B = 4
CB = 128
W_B = H_B * CB
CHUNK = 128
N_BUCKETS = 32
MAX_DISTANCE = 128
LN_EPS = 1e-5
SCALE = HD ** -0.5
NEG = -1e30
MASKED_BUCKET = N_BUCKETS

TM = 512
TQ = 512
TK = 512
T_PAD = 8
PAGES_PER_STEP = 8
VMEM_LIMIT = 56 * 1024 * 1024

_F32 = jnp.float32
_BF16 = jnp.bfloat16
_NT = (((1,), (1,)), ((), ()))


def _bucket_np(dist):
    n = np.maximum(dist, 0)
    max_exact = N_BUCKETS // 2
    nf = np.maximum(n, 1).astype(np.float32)
    large = max_exact + (np.log(nf / max_exact) / math.log(MAX_DISTANCE / max_exact)
                         * (N_BUCKETS - max_exact)).astype(np.int32)
    large = np.minimum(large, N_BUCKETS - 1)
    return np.where(n < max_exact, n, large).astype(np.int32)


def _silu(x):
    return x * jax.nn.sigmoid(x)


def _layer_norm_rows(x, g, b):
    mu = jnp.mean(x, axis=-1, keepdims=True)
    xc = x - mu
    var = jnp.mean(xc * xc, axis=-1, keepdims=True)
    return xc * lax.rsqrt(var + LN_EPS) * g + b


def _rms_norm_rows(x, g):
    return x * lax.rsqrt(jnp.mean(x * x, axis=-1, keepdims=True) + LN_EPS) * g


def _bias_kernel(tab_ref, idx_ref, o_ref):
    h = pl.program_id(0)
    idx = idx_ref[0]
    far = tab_ref[N_BUCKETS - 1, h]
    acc = jnp.full(idx.shape, NEG, _F32)
    for b in range(N_BUCKETS):
        acc = jnp.where(idx == b, tab_ref[b, h] - far, acc)
    o_ref[0, 0] = acc


def _bias_tiles(table, idx):
    n, r, c = idx.shape
    return pl.pallas_call(
        _bias_kernel,
        grid=(H_A, n),
        in_specs=[pl.BlockSpec(memory_space=pltpu.SMEM),
                  pl.BlockSpec((1, r, c), lambda h, i: (i, 0, 0))],
        out_specs=pl.BlockSpec((1, 1, r, c), lambda h, i: (h, i, 0, 0)),
        out_shape=jax.ShapeDtypeStruct((H_A, n, r, c), _F32),
        name="bias_tiles",
    )(table, idx)


def _prompt_proj_kernel(x_ref, wm_ref, wqt_ref, wvt_ref, lng_ref, lnb_ref, wt_ref, bs_ref,
                        kf_ref, vf_ref, kb_ref, qt_ref, vt_ref, ga_ref, g_ref):
    xb = x_ref[0].astype(_BF16)
    hm = jnp.dot(xb, wm_ref[...], preferred_element_type=_F32)
    k = hm[:, 0:W_A]
    v = hm[:, W_A:2 * W_A]
    for h in range(H_A):
        cols = slice(h * DK, (h + 1) * DK)
        kf_ref[0, pl.ds(h, TM, stride=H_A), :] = k[:, cols]
        vf_ref[0, pl.ds(h, TM, stride=H_A), :] = v[:, cols]
    kb_ref[0] = k.astype(_BF16)
    ga_ref[0] = hm[:, 2 * W_A:3 * W_A]
    qt_ref[0] = lax.dot_general(wqt_ref[...], xb, _NT, preferred_element_type=_F32).astype(_BF16)
    vt_ref[0] = lax.dot_general(wvt_ref[...], xb, _NT, preferred_element_type=_F32).astype(_BF16)
    o = 3 * W_A
    u = hm[:, o:o + W_B]
    vb = hm[:, o + W_B:o + 2 * W_B]
    gb = hm[:, o + 2 * W_B:o + 3 * W_B]
    vn = _layer_norm_rows(vb, lng_ref[...], lnb_ref[...]).astype(_BF16)
    gate = u * _silu(gb)
    for c in range(TM // CHUNK):
        rows = slice(c * CHUNK, (c + 1) * CHUNK)
        for h in range(H_B):
            cols = slice(h * CB, (h + 1) * CB)
            s = jnp.dot(wt_ref[h], vn[rows, cols], preferred_element_type=_F32) + bs_ref[h]
            g_ref[0, rows, cols] = (gate[rows, cols] * s).astype(_BF16)


def _prompt_proj(x, w_main, wqt, wvt, ln_g, ln_b, wt, bsb):
    b, s, d = x.shape
    row = lambda bi, si: (bi, si, 0)
    col = lambda bi, si: (bi, 0, si)
    const2 = lambda bi, si: (0, 0)
    const3 = lambda bi, si: (0, 0, 0)
    f32_heads = jax.ShapeDtypeStruct((b, s * H_A, DK), _F32)
    bf_rows = jax.ShapeDtypeStruct((b, s, W_A), _BF16)
    bf_cols = jax.ShapeDtypeStruct((b, W_A, s), _BF16)
    return pl.pallas_call(
        _prompt_proj_kernel,
        grid=(b, s // TM),
        in_specs=[pl.BlockSpec((1, TM, d), row),
                  pl.BlockSpec(w_main.shape, const2),
                  pl.BlockSpec(wqt.shape, const2),
                  pl.BlockSpec(wvt.shape, const2),
                  pl.BlockSpec(ln_g.shape, const2),
                  pl.BlockSpec(ln_b.shape, const2),
                  pl.BlockSpec(wt.shape, const3),
                  pl.BlockSpec(bsb.shape, const3)],
        out_specs=[pl.BlockSpec((1, TM * H_A, DK), row), pl.BlockSpec((1, TM * H_A, DK), row),
                   pl.BlockSpec((1, TM, W_A), row),
                   pl.BlockSpec((1, W_A, TM), col), pl.BlockSpec((1, W_A, TM), col),
                   pl.BlockSpec((1, TM, W_A), row), pl.BlockSpec((1, TM, W_B), row)],
        out_shape=[f32_heads, f32_heads, bf_rows, bf_cols, bf_cols,
                   jax.ShapeDtypeStruct((b, s, W_A), _F32), jax.ShapeDtypeStruct((b, s, W_B), _BF16)],
        compiler_params=pltpu.CompilerParams(dimension_semantics=("arbitrary", "arbitrary"),
                                             vmem_limit_bytes=VMEM_LIMIT),
        name="prompt_proj",
    )(x, w_main, wqt, wvt, ln_g, ln_b, wt, bsb)


def _prompt_attn_kernel(lam_ref, qt_ref, k_ref, vt_ref, bias_ref, ga_ref, sg_ref, a_ref,
                        qz_ref, m_ref, l_ref, acc_ref, *, lam_init):
    qi = pl.program_id(2)
    qt = qt_ref[0]
    row = lax.broadcasted_iota(jnp.int32, qt.shape, 0)
    zero = jnp.zeros_like(qt)
    qz_ref[:, 0:TQ] = jnp.where(row < HD, qt, zero)
    qz_ref[:, TQ:2 * TQ] = jnp.where(row >= HD, qt, zero)
    m_ref[...] = jnp.full(m_ref.shape, NEG, _F32)
    l_ref[...] = jnp.zeros(l_ref.shape, _F32)
    acc_ref[...] = jnp.zeros(acc_ref.shape, _F32)

    def tile(j, bias):
        start = pl.multiple_of(j * TK, TK)
        kblk = k_ref[0, pl.ds(start, TK), :]
        vblk = vt_ref[0, :, pl.ds(start, TK)]
        s = jnp.dot(kblk, qz_ref[...], preferred_element_type=_F32)
        for mp in range(2):
            sm = s[:, mp * TQ:(mp + 1) * TQ]
            if bias is not None:
                sm = sm + bias
            m_old = m_ref[mp]
            m_new = jnp.maximum(m_old, jnp.max(sm, axis=0, keepdims=True))
            alpha = jnp.exp(m_old - m_new)
            p = jnp.exp(sm - m_new)
            l_ref[mp] = alpha * l_ref[mp] + jnp.sum(p, axis=0, keepdims=True)
            acc_ref[mp] = acc_ref[mp] * alpha + jnp.dot(vblk, p.astype(_BF16), preferred_element_type=_F32)
            m_ref[mp] = m_new

    def far_body(j, carry):
        tile(j, None)
        return carry

    lax.fori_loop(0, jnp.maximum(qi - 1, 0), far_body, 0)

    @pl.when(qi >= 1)
    def _():
        tile(qi - 1, bias_ref[0, 0])

    tile(qi, bias_ref[0, 1])

    lam = lam_ref[0, 0]
    o_t = acc_ref[0] * (1.0 / l_ref[0]) - lam * (acc_ref[1] * (1.0 / l_ref[1]))
    o = o_t.T
    o = _rms_norm_rows(o, sg_ref[...]) * (1.0 - lam_init)
    a_ref[0] = (o * _silu(ga_ref[0])).astype(_BF16)


def _prompt_attn(lam, qt, kb, vt, bias, ga, subln_g, lam_init):
    b, s, _ = kb.shape
    return pl.pallas_call(
        functools.partial(_prompt_attn_kernel, lam_init=lam_init),
        grid=(b, H_A, s // TQ),
        in_specs=[pl.BlockSpec(memory_space=pltpu.SMEM),
                  pl.BlockSpec((1, DK, TQ), lambda bi, h, qi: (bi, h, qi)),
                  pl.BlockSpec((1, s, DK), lambda bi, h, qi: (bi, 0, h)),
                  pl.BlockSpec((1, DK, s), lambda bi, h, qi: (bi, h, 0)),
                  pl.BlockSpec((1, 2, TK, TQ), lambda bi, h, qi: (h, 0, 0, 0)),
                  pl.BlockSpec((1, TQ, DK), lambda bi, h, qi: (bi, qi, h)),
                  pl.BlockSpec(subln_g.shape, lambda bi, h, qi: (0, 0))],
        out_specs=pl.BlockSpec((1, TQ, DK), lambda bi, h, qi: (bi, qi, h)),
        out_shape=jax.ShapeDtypeStruct((b, s, W_A), _BF16),
        scratch_shapes=[pltpu.VMEM((DK, 2 * TQ), _BF16),
                        pltpu.VMEM((2, 1, TQ), _F32),
                        pltpu.VMEM((2, 1, TQ), _F32),
                        pltpu.VMEM((2, DK, TQ), _F32)],
        compiler_params=pltpu.CompilerParams(dimension_semantics=("arbitrary", "arbitrary", "arbitrary"),
                                             vmem_limit_bytes=VMEM_LIMIT),
        name="prompt_attn",
    )(lam, qt, kb, vt, bias, ga, subln_g)


def _out_proj_kernel(a_ref, g_ref, x_ref, wa_ref, wg_ref, lng_ref, lnb_ref, y_ref, *, alpha):
    out = (jnp.dot(a_ref[...], wa_ref[...], preferred_element_type=_F32)
           + jnp.dot(g_ref[...], wg_ref[...], preferred_element_type=_F32))
    z = alpha * x_ref[...] + out
    y_ref[...] = _layer_norm_rows(z, lng_ref[...], lnb_ref[...])


def _out_proj(a, g, x, wo_a, wo_g, ln_g, ln_b, alpha):
    n, d = x.shape
    tm = min(n, TM)
    row = lambda i: (i, 0)
    const = lambda i: (0, 0)
    return pl.pallas_call(
        functools.partial(_out_proj_kernel, alpha=alpha),
        grid=(n // tm,),
        in_specs=[pl.BlockSpec((tm, W_A), row), pl.BlockSpec((tm, W_B), row), pl.BlockSpec((tm, d), row),
                  pl.BlockSpec(wo_a.shape, const), pl.BlockSpec(wo_g.shape, const),
                  pl.BlockSpec(ln_g.shape, const), pl.BlockSpec(ln_b.shape, const)],
        out_specs=pl.BlockSpec((tm, d), row),
        out_shape=jax.ShapeDtypeStruct((n, d), _F32),
        compiler_params=pltpu.CompilerParams(dimension_semantics=("arbitrary",),
                                             vmem_limit_bytes=VMEM_LIMIT),
        name="out_proj",
    )(a, g, x, wo_a, wo_g, ln_g, ln_b)


def _sample_proj_kernel(x_ref, w_ref, lng_ref, lnb_ref, cs_ref, bs_ref, lp_ref,
                        q_ref, k_ref, v_ref, ga_ref, g_ref, vn_ref, lam_ref, *, lam_init, t_valid):
    rows = x_ref.shape[0]
    h = jnp.dot(x_ref[...].astype(_BF16), w_ref[...], preferred_element_type=_F32)
    q_ref[...] = h[:, 0:W_A]
    k_ref[...] = h[:, W_A:2 * W_A]
    v_ref[...] = h[:, 2 * W_A:3 * W_A]
    ga_ref[...] = h[:, 3 * W_A:4 * W_A]
    o = 4 * W_A
    u = h[:, o:o + W_B]
    vb = h[:, o + W_B:o + 2 * W_B]
    gb = h[:, o + 2 * W_B:o + 3 * W_B]
    vn = _layer_norm_rows(vb, lng_ref[...], lnb_ref[...])
    vn_ref[...] = vn
    vn3 = vn.reshape(rows // T_PAD, T_PAD, W_B)
    s3 = jnp.broadcast_to(bs_ref[...][None], vn3.shape)
    for j in range(t_valid):
        s3 = s3 + vn3[:, j:j + 1, :] * cs_ref[j][None]
    g_ref[...] = (u * s3.reshape(rows, W_B) * _silu(gb)).astype(_BF16)
    lp = lp_ref[...]
    e1 = jnp.exp(jnp.sum(lp[0:1] * lp[1:2], axis=-1, keepdims=True))
    e2 = jnp.exp(jnp.sum(lp[2:3] * lp[3:4], axis=-1, keepdims=True))
    lam_ref[...] = jnp.broadcast_to(e1 - e2 + lam_init, lam_ref.shape)


def _sample_proj(x, w, ln_g, ln_b, cs, bs8, lam_params, lam_init, t_valid):
    n, d = x.shape
    tm = min(n, 256)
    row = lambda i: (i, 0)
    const2 = lambda i: (0, 0)
    const3 = lambda i: (0, 0, 0)
    f32_rows = jax.ShapeDtypeStruct((n, W_A), _F32)
    return pl.pallas_call(
        functools.partial(_sample_proj_kernel, lam_init=lam_init, t_valid=t_valid),
        grid=(n // tm,),
        in_specs=[pl.BlockSpec((tm, d), row), pl.BlockSpec(w.shape, const2),
                  pl.BlockSpec(ln_g.shape, const2), pl.BlockSpec(ln_b.shape, const2),
                  pl.BlockSpec(cs.shape, const3), pl.BlockSpec(bs8.shape, const2),
                  pl.BlockSpec(lam_params.shape, const2)],
        out_specs=[pl.BlockSpec((tm, W_A), row)] * 4
                  + [pl.BlockSpec((tm, W_B), row), pl.BlockSpec((tm, W_B), row),
                     pl.BlockSpec((8, 128), const2)],
        out_shape=[f32_rows] * 4 + [jax.ShapeDtypeStruct((n, W_B), _BF16),
                                    jax.ShapeDtypeStruct((n, W_B), _F32),
                                    jax.ShapeDtypeStruct((8, 128), _F32)],
        compiler_params=pltpu.CompilerParams(dimension_semantics=("arbitrary",),
                                             vmem_limit_bytes=VMEM_LIMIT),
        name="sample_proj",
    )(x, w, ln_g, ln_b, cs, bs8, lam_params)


def _sample_attn_kernel(pt_ref, lam_ref, q_ref, kn_ref, vn_ref, ga_ref, hm_ref, neg_ref, bl_ref, bn_ref, sg_ref,
                        *rest, lam_init, n_steps):
    pg = PAGES_PER_STEP
    k_refs = rest[0:pg]
    v_refs = rest[pg:2 * pg]
    a_ref = rest[2 * pg]
    qm_ref, m_ref, l_ref, acc_ref = rest[2 * pg + 1:]
    j = pl.program_id(1)
    n_rows = 2 * H_A * T_PAD

    @pl.when(j == 0)
    def _():
        q8 = q_ref[0]
        per_head = [q8[:, h * DK:(h + 1) * DK] for h in range(H_A)]
        qm_ref[...] = jnp.concatenate(per_head + per_head, axis=0) * hm_ref[...]
        m_ref[...] = jnp.full(m_ref.shape, NEG, _F32)
        l_ref[...] = jnp.zeros(l_ref.shape, _F32)
        acc_ref[...] = jnp.zeros(acc_ref.shape, _F32)

    qm = qm_ref[...]
    neg = neg_ref[...]
    last = (j == n_steps - 1).astype(_F32)
    s = [lax.dot_general(qm, k_refs[i][...], _NT, preferred_element_type=_F32) + neg for i in range(pg)]
    s[pg - 1] = s[pg - 1] + last * bl_ref[...]
    smax = s[0]
    for i in range(1, pg):
        smax = jnp.maximum(smax, s[i])
    m_old = m_ref[...]
    m_new = jnp.maximum(m_old, jnp.max(smax, axis=-1, keepdims=True))
    alpha = jnp.exp(m_old - m_new)
    psum = None
    pv = None
    for i in range(pg):
        p = jnp.exp(s[i] - m_new)
        psum = p if psum is None else psum + p
        d = jnp.dot(p, v_refs[i][...], preferred_element_type=_F32)
        pv = d if pv is None else pv + d
    l_ref[...] = alpha * l_ref[...] + psum
    acc_ref[...] = alpha * acc_ref[...] + pv
    m_ref[...] = m_new

    @pl.when(j == n_steps - 1)
    def _():
        sn = lax.dot_general(qm, kn_ref[0], _NT, preferred_element_type=_F32) + bn_ref[...]
        m_o = m_ref[...]
        m_f = jnp.maximum(m_o, jnp.max(sn, axis=-1, keepdims=True))
        al = jnp.exp(m_o - m_f)
        pn = jnp.exp(sn - m_f)
        l_tot = al * jnp.sum(l_ref[...], axis=-1, keepdims=True) + jnp.sum(pn, axis=-1, keepdims=True)
        acc = al * acc_ref[...] + jnp.dot(pn, vn_ref[0], preferred_element_type=_F32)
        o_full = acc * (1.0 / l_tot)
        half = n_rows // 2
        od = o_full[0:half] - lam_ref[0, 0] * o_full[half:n_rows]
        ga = ga_ref[0]
        for h in range(H_A):
            cols = slice(h * DK, (h + 1) * DK)
            oh = _rms_norm_rows(od[h * T_PAD:(h + 1) * T_PAD], sg_ref[...]) * (1.0 - lam_init)
            a_ref[0, :, cols] = (oh * _silu(ga[:, cols])).astype(_BF16)


def _sample_attn(page_table, lam, q8, kn, vn, ga8, halfmask, negmask, bias_last, bias_new, subln_g, ck, cv,
                 lam_init):
    db, n_pages = page_table.shape
    pg = PAGES_PER_STEP
    n_steps = n_pages // pg
    page_rows = ck.shape[1]
    n_rows = 2 * H_A * T_PAD
    seq = lambda bi, j, pt: (bi, 0, 0)
    const = lambda bi, j, pt: (0, 0)

    def page_spec(i):
        return pl.BlockSpec((None, page_rows, DK), lambda bi, j, pt: (pt[bi, j * pg + i], 0, 0))

    grid_spec = pltpu.PrefetchScalarGridSpec(
        num_scalar_prefetch=1,
        grid=(db, n_steps),
        in_specs=[pl.BlockSpec(memory_space=pltpu.SMEM),
                  pl.BlockSpec((1, T_PAD, W_A), seq),
                  pl.BlockSpec((1, T_PAD * H_A, DK), seq), pl.BlockSpec((1, T_PAD * H_A, DK), seq),
                  pl.BlockSpec((1, T_PAD, W_A), seq),
                  pl.BlockSpec(halfmask.shape, const), pl.BlockSpec(negmask.shape, const),
                  pl.BlockSpec(bias_last.shape, const), pl.BlockSpec(bias_new.shape, const),
                  pl.BlockSpec(subln_g.shape, const)]
                 + [page_spec(i) for i in range(pg)] + [page_spec(i) for i in range(pg)],
        out_specs=pl.BlockSpec((1, T_PAD, W_A), seq),
        scratch_shapes=[pltpu.VMEM((n_rows, DK), _F32),
                        pltpu.VMEM((n_rows, 1), _F32),
                        pltpu.VMEM((n_rows, page_rows), _F32),
                        pltpu.VMEM((n_rows, DK), _F32)],
    )
    return pl.pallas_call(
        functools.partial(_sample_attn_kernel, lam_init=lam_init, n_steps=n_steps),
        grid_spec=grid_spec,
        out_shape=jax.ShapeDtypeStruct((db, T_PAD, W_A), _BF16),
        compiler_params=pltpu.CompilerParams(dimension_semantics=("arbitrary", "arbitrary"),
                                             vmem_limit_bytes=VMEM_LIMIT),
        name="sample_attn",
    )(page_table, lam, q8, kn, vn, ga8, halfmask, negmask, bias_last, bias_new, subln_g,
      *([ck] * pg), *([cv] * pg))


def kernel(x_prompt, x_sample, cache_k, cache_v, page_table, w_in, w_out, lambda_q1, lambda_k1, lambda_q2,
           lambda_k2, subln_g, rel_bias, sgu_ln_g, sgu_ln_b, sgu_w, sgu_b, post_ln_g, post_ln_b):
    depth = w_in.shape[0]
    assert depth == 1, "single trunk layer"
    b, s, d = x_prompt.shape
    db, t, _ = x_sample.shape
    n_pool, page = cache_k.shape[1], cache_k.shape[2]
    n_pages = page_table.shape[1]
    assert s % TQ == 0 and s % TM == 0 and TQ == TK and TQ >= MAX_DISTANCE
    assert t <= T_PAD and page >= MAX_DISTANCE and n_pages % PAGES_PER_STEP == 0
    lam_init = 0.8 - 0.6 * math.exp(-0.3 * 0)
    alpha = (2.0 * depth) ** 0.25

    w = w_in[0]
    wq_scaled = w[:, 0:W_A] * SCALE
    w_main = w[:, W_A:].astype(_BF16)
    wqt = wq_scaled.T.astype(_BF16)
    wvt = w[:, 2 * W_A:3 * W_A].T.astype(_BF16)
    w_sample = jnp.concatenate([wq_scaled, w[:, W_A:]], axis=1).astype(_BF16)
    wo_a = w_out[0, 0:W_A].astype(_BF16)
    wo_g = w_out[0, W_A:].astype(_BF16)
    ln_g = sgu_ln_g[0][None]
    ln_b = sgu_ln_b[0][None]
    pg_ = post_ln_g[0][None]
    pb_ = post_ln_b[0][None]
    sg = subln_g[0][None]
    tril = jnp.tril(jnp.ones((CHUNK, CHUNK), _F32))
    wt = (sgu_w[0] * tril).astype(_BF16)
    bsb = jnp.broadcast_to(sgu_b[0][:, :, None], (H_B, CHUNK, CB))
    wsmall = sgu_w[0][:, :T_PAD, :T_PAD] * tril[:T_PAD, :T_PAD]
    valid_t = (jnp.arange(T_PAD) < t).astype(_F32)
    cs = jnp.transpose(wsmall, (2, 1, 0)) * valid_t[None, :, None]
    cs = jnp.repeat(cs, CB, axis=2)
    bs8 = jnp.repeat(jnp.transpose(sgu_b[0][:, :T_PAD]) * valid_t[:, None], CB, axis=1)
    lam_params = jnp.concatenate([lambda_q1, lambda_k1, lambda_q2, lambda_k2], axis=0)

    kk = np.arange(TK)[:, None]
    qq = np.arange(TQ)[None, :]
    idx_sub = _bucket_np(TQ + qq - kk)
    idx_diag = np.where(qq >= kk, _bucket_np(qq - kk), MASKED_BUCKET)
    bias_prompt = _bias_tiles(rel_bias, jnp.asarray(np.stack([idx_sub, idx_diag]).astype(np.int32)))
    tt = np.arange(T_PAD)[:, None]
    ii = np.arange(page * H_A)[None, :] // H_A
    idx_last = _bucket_np(np.where(tt < t, page + tt - ii, MAX_DISTANCE))
    idx_new = np.where((ii <= tt) & (tt < t), _bucket_np(tt - ii), MASKED_BUCKET)
    bias_sample = _bias_tiles(rel_bias, jnp.asarray(np.stack([idx_last, idx_new]).astype(np.int32)))
    r = np.arange(2 * H_A * T_PAD)
    r_head = (r // T_PAD) % H_A
    r_map = r // (H_A * T_PAD)
    lane_head = np.arange(page * H_A) % H_A
    own_head = r_head[:, None] == lane_head[None, :]
    negmask = jnp.asarray(np.where(own_head, 0.0, NEG).astype(np.float32))
    halfmask = jnp.asarray((np.arange(DK)[None, :] // HD == r_map[:, None]).astype(np.float32))
    bias_last = jnp.tile(bias_sample[:, 0].reshape(H_A * T_PAD, page * H_A), (2, 1))
    bias_new = (jnp.tile(bias_sample[:, 1, :, :T_PAD * H_A].reshape(H_A * T_PAD, T_PAD * H_A), (2, 1))
                + negmask[:, :T_PAD * H_A])

    xs = jnp.pad(x_sample, ((0, 0), (0, T_PAD - t), (0, 0))).reshape(db * T_PAD, d)
    q8, k8, v8, ga8, g8, vn8, lam_tile = _sample_proj(xs, w_sample, ln_g, ln_b, cs, bs8, lam_params, lam_init, t)
    lam = lam_tile[0:1, 0:1]
    ck = cache_k[0].reshape(n_pool, page * H_A, DK)
    cv = cache_v[0].reshape(n_pool, page * H_A, DK)
    a8 = _sample_attn(page_table, lam, q8.reshape(db, T_PAD, W_A), k8.reshape(db, T_PAD * H_A, DK),
                      v8.reshape(db, T_PAD * H_A, DK), ga8.reshape(db, T_PAD, W_A), halfmask, negmask,
                      bias_last, bias_new, sg, ck, cv, lam_init)
    ys = _out_proj(a8.reshape(db * T_PAD, W_A), g8, xs, wo_a, wo_g, pg_, pb_, alpha)
    y_sample = ys.reshape(db, T_PAD, d)[:, :t]
    nks = k8.reshape(db, T_PAD, H_A, DK)[None, :, :t]
    nvs = v8.reshape(db, T_PAD, H_A, DK)[None, :, :t]
    nsv = vn8.reshape(db, T_PAD, H_B, CB)[None, :, :t]

    kf, vf, kb, qt, vt, ga, g = _prompt_proj(x_prompt, w_main, wqt, wvt, ln_g, ln_b, wt, bsb)
    a = _prompt_attn(lam, qt, kb, vt, bias_prompt, ga, sg, lam_init)
    yp = _out_proj(a.reshape(b * s, W_A), g.reshape(b * s, W_B), x_prompt.reshape(b * s, d),
                   wo_a, wo_g, pg_, pb_, alpha)
    y_prompt = yp.reshape(b, s, d)
    nkp = kf.reshape(1, b, s, H_A, DK)
    nvp = vf.reshape(1, b, s, H_A, DK)
    return (y_prompt, y_sample, nkp, nvp, nks, nvs, nsv)
```

```python
import functools
import math

import numpy as np
import jax
import jax.numpy as jnp
from jax import lax
from jax.experimental import pallas as pl
from jax.experimental.pallas import tpu as pltpu

H_A = 4
HD = 64
DK = 2 * HD
W_A = H_A * DK
H_B = 4
CB = 128
W_B = H_B * CB
CHUNK = 128
N_BUCKETS = 32
MAX_DISTANCE = 128
LN_EPS = 1e-5
SCALE = HD ** -0.5
LOG2E = math.log2(math.e)
NEG = -1e30
MASKED_BUCKET = N_BUCKETS

TM = 512
TQ = 512
TK = 512
QH = TQ // 2
ONES_ROWS = 16
VA = DK + ONES_ROWS
T_PAD = 8
PAGES_PER_GROUP = 4
RING_GROUPS = 4
VMEM_LIMIT = 56 * 1024 * 1024

_F32 = jnp.float32
_BF16 = jnp.bfloat16
_NT = (((1,), (1,)), ((), ()))


def _bucket_np(dist):
    n = np.maximum(dist, 0)
    max_exact = N_BUCKETS // 2
    nf = np.maximum(n, 1).astype(np.float32)
    large = max_exact + (np.log(nf / max_exact) / math.log(MAX_DISTANCE / max_exact)
                         * (N_BUCKETS - max_exact)).astype(np.int32)
    large = np.minimum(large, N_BUCKETS - 1)
    return np.where(n < max_exact, n, large).astype(np.int32)


def _silu(x):
    return x * jax.nn.sigmoid(x)


def _layer_norm_rows(x, g, b):
    mu = jnp.mean(x, axis=-1, keepdims=True)
    xc = x - mu
    var = jnp.mean(xc * xc, axis=-1, keepdims=True)
    return xc * lax.rsqrt(var + LN_EPS) * g + b


def _rms_norm_rows(x, g):
    return x * lax.rsqrt(jnp.mean(x * x, axis=-1, keepdims=True) + LN_EPS) * g


def _bias_kernel(tab_ref, idx_ref, o_ref):
    h = pl.program_id(0)
    idx = idx_ref[0]
    far = tab_ref[N_BUCKETS - 1, h]
    acc = jnp.full(idx.shape, NEG, _F32)
    for b in range(N_BUCKETS):
        acc = jnp.where(idx == b, (tab_ref[b, h] - far) * LOG2E, acc)
    o_ref[0, 0] = acc


def _bias_tiles(table, idx):
    n, r, c = idx.shape
    return pl.pallas_call(
        _bias_kernel,
        grid=(H_A, n),
        in_specs=[pl.BlockSpec(memory_space=pltpu.SMEM),
                  pl.BlockSpec((1, r, c), lambda h, i: (i, 0, 0))],
        out_specs=pl.BlockSpec((1, 1, r, c), lambda h, i: (h, i, 0, 0)),
        out_shape=jax.ShapeDtypeStruct((H_A, n, r, c), _F32),
        name="bias_tiles",
    )(table, idx)


def _prompt_proj_kernel(x_ref, wm_ref, wqt_ref, wvt_ref, lng_ref, lnb_ref, wt_ref, bs_ref,
                        kf_ref, vf_ref, kb_ref, qt_ref, vt_ref, ga_ref, g_ref):
    xb = x_ref[0].astype(_BF16)
    hm = jnp.dot(xb, wm_ref[...], preferred_element_type=_F32)
    k = hm[:, 0:W_A]
    v = hm[:, W_A:2 * W_A]
    for h in range(H_A):
        cols = slice(h * DK, (h + 1) * DK)
        kf_ref[0, pl.ds(h, TM, stride=H_A), :] = k[:, cols]
        vf_ref[0, pl.ds(h, TM, stride=H_A), :] = v[:, cols]
    kb_ref[0] = k.astype(_BF16)
    ga_ref[0] = hm[:, 2 * W_A:3 * W_A]
    qt_ref[0] = lax.dot_general(wqt_ref[...], xb, _NT, preferred_element_type=_F32).astype(_BF16)
    vt = lax.dot_general(wvt_ref[...], xb, _NT, preferred_element_type=_F32).astype(_BF16)
    for h in range(H_A):
        vt_ref[0, h, 0:DK, :] = vt[h * DK:(h + 1) * DK]
        vt_ref[0, h, DK:VA, :] = jnp.ones((ONES_ROWS, TM), _BF16)
    o = 3 * W_A
    u = hm[:, o:o + W_B]
    vb = hm[:, o + W_B:o + 2 * W_B]
    gb = hm[:, o + 2 * W_B:o + 3 * W_B]
    vn = _layer_norm_rows(vb, lng_ref[...], lnb_ref[...]).astype(_BF16)
    gate = u * _silu(gb)
    for c in range(TM // CHUNK):
        rows = slice(c * CHUNK, (c + 1) * CHUNK)
        for h in range(H_B):
            cols = slice(h * CB, (h + 1) * CB)
            s = jnp.dot(wt_ref[h], vn[rows, cols], preferred_element_type=_F32) + bs_ref[h]
            g_ref[0, rows, cols] = (gate[rows, cols] * s).astype(_BF16)


def _prompt_proj(x, w_main, wqt, wvt, ln_g, ln_b, wt, bsb):
    b, s, d = x.shape
    row = lambda bi, si: (bi, si, 0)
    col = lambda bi, si: (bi, 0, si)
    const2 = lambda bi, si: (0, 0)
    const3 = lambda bi, si: (0, 0, 0)
    f32_heads = jax.ShapeDtypeStruct((b, s * H_A, DK), _F32)
    bf_rows = jax.ShapeDtypeStruct((b, s, W_A), _BF16)
    bf_cols = jax.ShapeDtypeStruct((b, W_A, s), _BF16)
    return pl.pallas_call(
        _prompt_proj_kernel,
        grid=(b, s // TM),
        in_specs=[pl.BlockSpec((1, TM, d), row),
                  pl.BlockSpec(w_main.shape, const2),
                  pl.BlockSpec(wqt.shape, const2),
                  pl.BlockSpec(wvt.shape, const2),
                  pl.BlockSpec(ln_g.shape, const2),
                  pl.BlockSpec(ln_b.shape, const2),
                  pl.BlockSpec(wt.shape, const3),
                  pl.BlockSpec(bsb.shape, const3)],
        out_specs=[pl.BlockSpec((1, TM * H_A, DK), row), pl.BlockSpec((1, TM * H_A, DK), row),
                   pl.BlockSpec((1, TM, W_A), row),
                   pl.BlockSpec((1, W_A, TM), col),
                   pl.BlockSpec((1, H_A, VA, TM), lambda bi, si: (bi, 0, 0, si)),
                   pl.BlockSpec((1, TM, W_A), row), pl.BlockSpec((1, TM, W_B), row)],
        out_shape=[f32_heads, f32_heads, bf_rows, bf_cols,
                   jax.ShapeDtypeStruct((b, H_A, VA, s), _BF16),
                   jax.ShapeDtypeStruct((b, s, W_A), _F32), jax.ShapeDtypeStruct((b, s, W_B), _BF16)],
        compiler_params=pltpu.CompilerParams(dimension_semantics=("arbitrary", "arbitrary"),
                                             vmem_limit_bytes=VMEM_LIMIT),
        name="prompt_proj",
    )(x, w_main, wqt, wvt, ln_g, ln_b, wt, bsb)


def _prompt_attn_kernel(lam_ref, qt_ref, k_ref, vt_ref, bias_ref, ga_ref, sg_ref, a_ref,
                        qz_ref, sa_ref, sb_ref, m_ref, l_ref, acc_ref, *, lam_init):
    qi = pl.program_id(2)
    qt = qt_ref[0]
    row = lax.broadcasted_iota(jnp.int32, qt.shape, 0)
    zero = jnp.zeros_like(qt)
    qz_ref[:, 0:TQ] = jnp.where(row < HD, qt, zero)
    qz_ref[:, TQ:2 * TQ] = jnp.where(row >= HD, qt, zero)
    m_ref[...] = jnp.full(m_ref.shape, NEG, _F32)
    l_ref[...] = jnp.zeros(l_ref.shape, _F32)
    acc_ref[...] = jnp.zeros(acc_ref.shape, _F32)

    def scores(j, dst_ref):
        start = pl.multiple_of(j * TK, TK)
        dst_ref[...] = jnp.dot(k_ref[0, pl.ds(start, TK), :], qz_ref[...], preferred_element_type=_F32)

    def consume(src_ref, j, bias_idx):
        start = pl.multiple_of(j * TK, TK)
        vblk = vt_ref[0, 0, :, pl.ds(start, TK)]
        for mp in range(2):
            for hf in range(TQ // QH):
                qs = slice(hf * QH, (hf + 1) * QH)
                sm = src_ref[:, mp * TQ + hf * QH:mp * TQ + (hf + 1) * QH]
                if bias_idx is not None:
                    sm = sm + bias_ref[0, bias_idx, :, qs]
                m_old = m_ref[mp, :, qs]
                m_new = jnp.maximum(m_old, jnp.max(sm, axis=0, keepdims=True))
                alpha = jnp.exp2(m_old - m_new)
                p = jnp.exp2(sm - m_new).astype(_BF16)
                pv = jnp.dot(vblk, p, preferred_element_type=_F32)
                acc_ref[mp, :, qs] = acc_ref[mp, :, qs] * alpha + pv[0:DK]
                l_ref[mp, :, qs] = l_ref[mp, :, qs] * alpha + pv[DK:DK + 1]
                m_ref[mp, :, qs] = m_new

    n_far = jnp.maximum(qi - 1, 0)
    odd = jnp.bitwise_and(n_far, 1)
    first_in_b = jnp.logical_or(qi == 0, odd == 1)

    @pl.when(first_in_b)
    def _():
        scores(0, sb_ref)

    @pl.when(jnp.logical_not(first_in_b))
    def _():
        scores(0, sa_ref)

    @pl.when(jnp.logical_and(qi >= 1, odd == 1))
    def _():
        scores(1, sa_ref)
        consume(sb_ref, 0, None)

    def far_pair(i, carry):
        t0 = odd + 2 * i
        scores(t0 + 1, sb_ref)
        consume(sa_ref, t0, None)
        scores(t0 + 2, sa_ref)
        consume(sb_ref, t0 + 1, None)
        return carry

    lax.fori_loop(0, lax.shift_right_logical(n_far, 1), far_pair, 0)

    @pl.when(qi >= 1)
    def _():
        scores(qi, sb_ref)
        consume(sa_ref, qi - 1, 0)

    consume(sb_ref, qi, 1)

    lam = lam_ref[0, 0]
    o_t = acc_ref[0] * (1.0 / l_ref[0]) - lam * (acc_ref[1] * (1.0 / l_ref[1]))
    o = o_t.T
    o = _rms_norm_rows(o, sg_ref[...]) * (1.0 - lam_init)
    a_ref[0] = (o * _silu(ga_ref[0])).astype(_BF16)


def _prompt_attn(lam, qt, kb, vt, bias, ga, subln_g, lam_init):
    b, s, _ = kb.shape
    return pl.pallas_call(
        functools.partial(_prompt_attn_kernel, lam_init=lam_init),
        grid=(b, H_A, s // TQ),
        in_specs=[pl.BlockSpec(memory_space=pltpu.SMEM),
                  pl.BlockSpec((1, DK, TQ), lambda bi, h, qi: (bi, h, qi)),
                  pl.BlockSpec((1, s, DK), lambda bi, h, qi: (bi, 0, h)),
                  pl.BlockSpec((1, 1, VA, s), lambda bi, h, qi: (bi, h, 0, 0)),
                  pl.BlockSpec((1, 2, TK, TQ), lambda bi, h, qi: (h, 0, 0, 0)),
                  pl.BlockSpec((1, TQ, DK), lambda bi, h, qi: (bi, qi, h)),
                  pl.BlockSpec(subln_g.shape, lambda bi, h, qi: (0, 0))],
        out_specs=pl.BlockSpec((1, TQ, DK), lambda bi, h, qi: (bi, qi, h)),
        out_shape=jax.ShapeDtypeStruct((b, s, W_A), _BF16),
        scratch_shapes=[pltpu.VMEM((DK, 2 * TQ), _BF16),
                        pltpu.VMEM((TK, 2 * TQ), _F32),
                        pltpu.VMEM((TK, 2 * TQ), _F32),
                        pltpu.VMEM((2, 1, TQ), _F32),
                        pltpu.VMEM((2, 1, TQ), _F32),
                        pltpu.VMEM((2, DK, TQ), _F32)],
        compiler_params=pltpu.CompilerParams(dimension_semantics=("arbitrary", "arbitrary", "arbitrary"),
                                             vmem_limit_bytes=VMEM_LIMIT),
        name="prompt_attn",
    )(lam, qt, kb, vt, bias, ga, subln_g)


def _out_proj_kernel(a_ref, g_ref, x_ref, wa_ref, wg_ref, lng_ref, lnb_ref, y_ref, *, alpha):
    out = (jnp.dot(a_ref[...], wa_ref[...], preferred_element_type=_F32)
           + jnp.dot(g_ref[...], wg_ref[...], preferred_element_type=_F32))
    z = alpha * x_ref[...] + out
    y_ref[...] = _layer_norm_rows(z, lng_ref[...], lnb_ref[...])


def _out_proj(a, g, x, wo_a, wo_g, ln_g, ln_b, alpha):
    n, d = x.shape
    tm = min(n, TM)
    row = lambda i: (i, 0)
    const = lambda i: (0, 0)
    return pl.pallas_call(
        functools.partial(_out_proj_kernel, alpha=alpha),
        grid=(n // tm,),
        in_specs=[pl.BlockSpec((tm, W_A), row), pl.BlockSpec((tm, W_B), row), pl.BlockSpec((tm, d), row),
                  pl.BlockSpec(wo_a.shape, const), pl.BlockSpec(wo_g.shape, const),
                  pl.BlockSpec(ln_g.shape, const), pl.BlockSpec(ln_b.shape, const)],
        out_specs=pl.BlockSpec((tm, d), row),
        out_shape=jax.ShapeDtypeStruct((n, d), _F32),
        compiler_params=pltpu.CompilerParams(dimension_semantics=("arbitrary",),
                                             vmem_limit_bytes=VMEM_LIMIT),
        name="out_proj",
    )(a, g, x, wo_a, wo_g, ln_g, ln_b)


def _sample_proj_kernel(x_ref, w_ref, lng_ref, lnb_ref, cs_ref, bs_ref, lp_ref,
                        q_ref, k_ref, v_ref, ga_ref, g_ref, vn_ref, lam_ref, *, lam_init, t_valid):
    rows = x_ref.shape[0]
    h = jnp.dot(x_ref[...].astype(_BF16), w_ref[...], preferred_element_type=_F32)
    q_ref[...] = h[:, 0:W_A]
    k_ref[...] = h[:, W_A:2 * W_A]
    v_ref[...] = h[:, 2 * W_A:3 * W_A]
    ga_ref[...] = h[:, 3 * W_A:4 * W_A]
    o = 4 * W_A
    u = h[:, o:o + W_B]
    vb = h[:, o + W_B:o + 2 * W_B]
    gb = h[:, o + 2 * W_B:o + 3 * W_B]
    vn = _layer_norm_rows(vb, lng_ref[...], lnb_ref[...])
    vn_ref[...] = vn
    vn3 = vn.reshape(rows // T_PAD, T_PAD, W_B)
    s3 = jnp.broadcast_to(bs_ref[...][None], vn3.shape)
    for j in range(t_valid):
        s3 = s3 + vn3[:, j:j + 1, :] * cs_ref[j][None]
    g_ref[...] = (u * s3.reshape(rows, W_B) * _silu(gb)).astype(_BF16)
    lp = lp_ref[...]
    e1 = jnp.exp(jnp.sum(lp[0:1] * lp[1:2], axis=-1, keepdims=True))
    e2 = jnp.exp(jnp.sum(lp[2:3] * lp[3:4], axis=-1, keepdims=True))
    lam_ref[...] = jnp.broadcast_to(e1 - e2 + lam_init, lam_ref.shape)


def _sample_proj(x, w, ln_g, ln_b, cs, bs8, lam_params, lam_init, t_valid):
    n, d = x.shape
    tm = min(n, 256)
    row = lambda i: (i, 0)
    const2 = lambda i: (0, 0)
    const3 = lambda i: (0, 0, 0)
    f32_rows = jax.ShapeDtypeStruct((n, W_A), _F32)
    return pl.pallas_call(
        functools.partial(_sample_proj_kernel, lam_init=lam_init, t_valid=t_valid),
        grid=(n // tm,),
        in_specs=[pl.BlockSpec((tm, d), row), pl.BlockSpec(w.shape, const2),
                  pl.BlockSpec(ln_g.shape, const2), pl.BlockSpec(ln_b.shape, const2),
                  pl.BlockSpec(cs.shape, const3), pl.BlockSpec(bs8.shape, const2),
                  pl.BlockSpec(lam_params.shape, const2)],
        out_specs=[pl.BlockSpec((tm, W_A), row)] * 4
                  + [pl.BlockSpec((tm, W_B), row), pl.BlockSpec((tm, W_B), row),
                     pl.BlockSpec((8, 128), const2)],
        out_shape=[f32_rows] * 4 + [jax.ShapeDtypeStruct((n, W_B), _BF16),
                                    jax.ShapeDtypeStruct((n, W_B), _F32),
                                    jax.ShapeDtypeStruct((8, 128), _F32)],
        compiler_params=pltpu.CompilerParams(dimension_semantics=("arbitrary",),
                                             vmem_limit_bytes=VMEM_LIMIT),
        name="sample_proj",
    )(x, w, ln_g, ln_b, cs, bs8, lam_params)


def _sample_attn_kernel(pt_ref, lam_ref, q_ref, kn_ref, vn_ref, ga_ref, hm_ref, neg_ref, bl_ref, bn_ref, sg_ref,
                        ck_ref, cv_ref, a_ref, kbuf, vbuf, ksem, vsem, qm_ref, s0_ref, s1_ref, m_ref, l_ref, acc_ref,
                        *, lam_init, n_groups, n_seq):
    gp = PAGES_PER_GROUP
    b = pl.program_id(0)
    n_rows = 2 * H_A * T_PAD
    page_rows = kbuf.shape[1]

    def group_copies(hbm_ref, buf, sem, seq, grp):
        slot = jnp.bitwise_and(grp, RING_GROUPS - 1)
        return [pltpu.make_async_copy(hbm_ref.at[pt_ref[seq, grp * gp + i]], buf.at[slot * gp + i], sem.at[slot])
                for i in range(gp)]

    def start_ahead(hbm_ref, buf, sem, grp):
        ahead = jnp.asarray(grp + RING_GROUPS, jnp.int32)
        wraps = (ahead >= n_groups).astype(jnp.int32)
        seq = b + wraps
        grp2 = ahead - wraps * n_groups

        @pl.when(seq < n_seq)
        def _():
            for c in group_copies(hbm_ref, buf, sem, seq, grp2):
                c.start()

    def wait_group(hbm_ref, buf, sem, grp):
        for c in group_copies(hbm_ref, buf, sem, b, grp):
            c.wait()

    @pl.when(b == 0)
    def _():
        for g in range(RING_GROUPS):
            for c in group_copies(ck_ref, kbuf, ksem, 0, g) + group_copies(cv_ref, vbuf, vsem, 0, g):
                c.start()

    q8 = q_ref[0]
    per_head = [q8[:, h * DK:(h + 1) * DK] for h in range(H_A)]
    qm_ref[...] = jnp.concatenate(per_head + per_head, axis=0) * hm_ref[...]
    m_ref[...] = jnp.full(m_ref.shape, NEG, _F32)
    l_ref[...] = jnp.zeros(l_ref.shape, _F32)
    acc_ref[...] = jnp.zeros(acc_ref.shape, _F32)

    def scores(grp, dst_ref, with_last_bias):
        slot = jnp.bitwise_and(grp, RING_GROUPS - 1)
        qm = qm_ref[...]
        for i in range(gp):
            s = lax.dot_general(qm, kbuf[slot * gp + i], _NT, preferred_element_type=_F32) + neg_ref[...]
            if with_last_bias and i == gp - 1:
                s = s + bl_ref[...]
            dst_ref[:, i * page_rows:(i + 1) * page_rows] = s

    def consume(grp, src_ref):
        slot = jnp.bitwise_and(grp, RING_GROUPS - 1)
        smax = src_ref[:, 0:page_rows]
        for i in range(1, gp):
            smax = jnp.maximum(smax, src_ref[:, i * page_rows:(i + 1) * page_rows])
        m_old = m_ref[...]
        m_new = jnp.maximum(m_old, jnp.max(smax, axis=-1, keepdims=True))
        alpha = jnp.exp2(m_old - m_new)
        l = alpha * l_ref[...]
        acc = alpha * acc_ref[...]
        for i in range(gp):
            p = jnp.exp2(src_ref[:, i * page_rows:(i + 1) * page_rows] - m_new)
            l = l + p
            acc = acc + jnp.dot(p, vbuf[slot * gp + i], preferred_element_type=_F32)
        l_ref[...] = l
        acc_ref[...] = acc
        m_ref[...] = m_new

    def block(grp, dst_ref, src_ref, with_last_bias=False):
        wait_group(ck_ref, kbuf, ksem, grp)
        wait_group(cv_ref, vbuf, vsem, grp - 1)
        scores(grp, dst_ref, with_last_bias)
        consume(grp - 1, src_ref)
        start_ahead(ck_ref, kbuf, ksem, grp)
        start_ahead(cv_ref, vbuf, vsem, grp - 1)

    wait_group(ck_ref, kbuf, ksem, 0)
    scores(0, s0_ref, False)
    start_ahead(ck_ref, kbuf, ksem, 0)

    def pair(i, carry):
        g = 2 * i + 1
        block(g, s1_ref, s0_ref)
        block(g + 1, s0_ref, s1_ref)
        return carry

    lax.fori_loop(0, (n_groups - 2) // 2, pair, 0)
    block(n_groups - 1, s1_ref, s0_ref, with_last_bias=True)
    wait_group(cv_ref, vbuf, vsem, n_groups - 1)
    consume(n_groups - 1, s1_ref)
    start_ahead(cv_ref, vbuf, vsem, n_groups - 1)

    sn = lax.dot_general(qm_ref[...], kn_ref[0], _NT, preferred_element_type=_F32) + bn_ref[...]
    m_o = m_ref[...]
    m_f = jnp.maximum(m_o, jnp.max(sn, axis=-1, keepdims=True))
    al = jnp.exp2(m_o - m_f)
    pn = jnp.exp2(sn - m_f)
    l_tot = al * jnp.sum(l_ref[...], axis=-1, keepdims=True) + jnp.sum(pn, axis=-1, keepdims=True)
    acc = al * acc_ref[...] + jnp.dot(pn, vn_ref[0], preferred_element_type=_F32)
    o_full = acc * (1.0 / l_tot)
    half = n_rows // 2
    od = o_full[0:half] - lam_ref[0, 0] * o_full[half:n_rows]
    ga = ga_ref[0]
    for h in range(H_A):
        cols = slice(h * DK, (h + 1) * DK)
        oh = _rms_norm_rows(od[h * T_PAD:(h + 1) * T_PAD], sg_ref[...]) * (1.0 - lam_init)
        a_ref[0, :, cols] = (oh * _silu(ga[:, cols])).astype(_BF16)


def _sample_attn(page_table, lam, q8, kn, vn, ga8, halfmask, negmask, bias_last, bias_new, subln_g, ck, cv,
                 lam_init):
    db, n_pages = page_table.shape
    gp = PAGES_PER_GROUP
    n_groups = n_pages // gp
    assert n_pages % gp == 0 and n_groups % RING_GROUPS == 0 and n_groups % 2 == 0
    page_rows = ck.shape[1]
    n_rows = 2 * H_A * T_PAD
    seq = lambda bi, pt: (bi, 0, 0)
    const = lambda bi, pt: (0, 0)
    grid_spec = pltpu.PrefetchScalarGridSpec(
        num_scalar_prefetch=1,
        grid=(db,),
        in_specs=[pl.BlockSpec(memory_space=pltpu.SMEM),
                  pl.BlockSpec((1, T_PAD, W_A), seq),
                  pl.BlockSpec((1, T_PAD * H_A, DK), seq), pl.BlockSpec((1, T_PAD * H_A, DK), seq),
                  pl.BlockSpec((1, T_PAD, W_A), seq),
                  pl.BlockSpec(halfmask.shape, const), pl.BlockSpec(negmask.shape, const),
                  pl.BlockSpec(bias_last.shape, const), pl.BlockSpec(bias_new.shape, const),
                  pl.BlockSpec(subln_g.shape, const),
                  pl.BlockSpec(memory_space=pl.ANY), pl.BlockSpec(memory_space=pl.ANY)],
        out_specs=pl.BlockSpec((1, T_PAD, W_A), seq),
        scratch_shapes=[pltpu.VMEM((RING_GROUPS * gp, page_rows, DK), _F32),
                        pltpu.VMEM((RING_GROUPS * gp, page_rows, DK), _F32),
                        pltpu.SemaphoreType.DMA((RING_GROUPS,)),
                        pltpu.SemaphoreType.DMA((RING_GROUPS,)),
                        pltpu.VMEM((n_rows, DK), _F32),
                        pltpu.VMEM((n_rows, gp * page_rows), _F32),
                        pltpu.VMEM((n_rows, gp * page_rows), _F32),
                        pltpu.VMEM((n_rows, 1), _F32),
                        pltpu.VMEM((n_rows, page_rows), _F32),
                        pltpu.VMEM((n_rows, DK), _F32)],
    )
    return pl.pallas_call(
        functools.partial(_sample_attn_kernel, lam_init=lam_init, n_groups=n_groups, n_seq=db),
        grid_spec=grid_spec,
        out_shape=jax.ShapeDtypeStruct((db, T_PAD, W_A), _BF16),
        compiler_params=pltpu.CompilerParams(dimension_semantics=("arbitrary",),
                                             vmem_limit_bytes=VMEM_LIMIT),
        name="sample_attn",
    )(page_table, lam, q8, kn, vn, ga8, halfmask, negmask, bias_last, bias_new, subln_g, ck, cv)


def kernel(x_prompt, x_sample, cache_k, cache_v, page_table, w_in, w_out, lambda_q1, lambda_k1, lambda_q2,
           lambda_k2, subln_g, rel_bias, sgu_ln_g, sgu_ln_b, sgu_w, sgu_b, post_ln_g, post_ln_b):
    depth = w_in.shape[0]
    assert depth == 1, "single trunk layer"
    b, s, d = x_prompt.shape
    db, t, _ = x_sample.shape
    n_pool, page = cache_k.shape[1], cache_k.shape[2]
    n_pages = page_table.shape[1]
    assert s % TQ == 0 and s % TM == 0 and TQ == TK and TQ >= MAX_DISTANCE
    assert t <= T_PAD and page >= MAX_DISTANCE
    lam_init = 0.8 - 0.6 * math.exp(-0.3 * 0)
    alpha = (2.0 * depth) ** 0.25

    w = w_in[0]
    wq_scaled = w[:, 0:W_A] * (SCALE * LOG2E)
    w_main = w[:, W_A:].astype(_BF16)
    wqt = wq_scaled.T.astype(_BF16)
    wvt = w[:, 2 * W_A:3 * W_A].T.astype(_BF16)
    w_sample = jnp.concatenate([wq_scaled, w[:, W_A:]], axis=1).astype(_BF16)
    wo_a = w_out[0, 0:W_A].astype(_BF16)
    wo_g = w_out[0, W_A:].astype(_BF16)
    ln_g = sgu_ln_g[0][None]
    ln_b = sgu_ln_b[0][None]
    pg_ = post_ln_g[0][None]
    pb_ = post_ln_b[0][None]
    sg = subln_g[0][None]
    tril = jnp.tril(jnp.ones((CHUNK, CHUNK), _F32))
    wt = (sgu_w[0] * tril).astype(_BF16)
    bsb = jnp.broadcast_to(sgu_b[0][:, :, None], (H_B, CHUNK, CB))
    wsmall = sgu_w[0][:, :T_PAD, :T_PAD] * tril[:T_PAD, :T_PAD]
    valid_t = (jnp.arange(T_PAD) < t).astype(_F32)
    cs = jnp.transpose(wsmall, (2, 1, 0)) * valid_t[None, :, None]
    cs = jnp.repeat(cs, CB, axis=2)
    bs8 = jnp.repeat(jnp.transpose(sgu_b[0][:, :T_PAD]) * valid_t[:, None], CB, axis=1)
    lam_params = jnp.concatenate([lambda_q1, lambda_k1, lambda_q2, lambda_k2], axis=0)

    kk = np.arange(TK)[:, None]
    qq = np.arange(TQ)[None, :]
    idx_sub = _bucket_np(TQ + qq - kk)
    idx_diag = np.where(qq >= kk, _bucket_np(qq - kk), MASKED_BUCKET)
    bias_prompt = _bias_tiles(rel_bias, jnp.asarray(np.stack([idx_sub, idx_diag]).astype(np.int32)))
    tt = np.arange(T_PAD)[:, None]
    ii = np.arange(page * H_A)[None, :] // H_A
    idx_last = _bucket_np(np.where(tt < t, page + tt - ii, MAX_DISTANCE))
    idx_new = np.where((ii <= tt) & (tt < t), _bucket_np(tt - ii), MASKED_BUCKET)
    bias_sample = _bias_tiles(rel_bias, jnp.asarray(np.stack([idx_last, idx_new]).astype(np.int32)))
    r = np.arange(2 * H_A * T_PAD)
    r_head = (r // T_PAD) % H_A
    r_map = r // (H_A * T_PAD)
    lane_head = np.arange(page * H_A) % H_A
    own_head = r_head[:, None] == lane_head[None, :]
    negmask = jnp.asarray(np.where(own_head, 0.0, NEG).astype(np.float32))
    halfmask = jnp.asarray((np.arange(DK)[None, :] // HD == r_map[:, None]).astype(np.float32))
    bias_last = jnp.tile(bias_sample[:, 0].reshape(H_A * T_PAD, page * H_A), (2, 1))
    bias_new = (jnp.tile(bias_sample[:, 1, :, :T_PAD * H_A].reshape(H_A * T_PAD, T_PAD * H_A), (2, 1))
                + negmask[:, :T_PAD * H_A])

    xs = jnp.pad(x_sample, ((0, 0), (0, T_PAD - t), (0, 0))).reshape(db * T_PAD, d)
    q8, k8, v8, ga8, g8, vn8, lam_tile = _sample_proj(xs, w_sample, ln_g, ln_b, cs, bs8, lam_params, lam_init, t)
    lam = lam_tile[0:1, 0:1]
    ck = cache_k[0].reshape(n_pool, page * H_A, DK)
    cv = cache_v[0].reshape(n_pool, page * H_A, DK)
    a8 = _sample_attn(page_table, lam, q8.reshape(db, T_PAD, W_A), k8.reshape(db, T_PAD * H_A, DK),
                      v8.reshape(db, T_PAD * H_A, DK), ga8.reshape(db, T_PAD, W_A), halfmask, negmask,
                      bias_last, bias_new, sg, ck, cv, lam_init)
    ys = _out_proj(a8.reshape(db * T_PAD, W_A), g8, xs, wo_a, wo_g, pg_, pb_, alpha)
    y_sample = ys.reshape(db, T_PAD, d)[:, :t]
    nks = k8.reshape(db, T_PAD, H_A, DK)[None, :, :t]
    nvs = v8.reshape(db, T_PAD, H_A, DK)[None, :, :t]
    nsv = vn8.reshape(db, T_PAD, H_B, CB)[None, :, :t]

    kf, vf, kb, qt, vt, ga, g = _prompt_proj(x_prompt, w_main, wqt, wvt, ln_g, ln_b, wt, bsb)
    a = _prompt_attn(lam, qt, kb, vt, bias_prompt, ga, sg, lam_init)
    yp = _out_proj(a.reshape(b * s, W_A), g.reshape(b * s, W_B), x_prompt.reshape(b * s, d),
                   wo_a, wo_g, pg_, pb_, alpha)
    y_prompt = yp.reshape(b, s, d)
    nkp = kf.reshape(1, b, s, H_A, DK)
    nvp = vf.reshape(1, b, s, H_A, DK)
    return (y_prompt, y_sample, nkp, nvp, nks, nvs, nsv)
```

```python
import functools
import math

import numpy as np
import jax
import jax.numpy as jnp
from jax import lax
from jax.experimental import pallas as pl
from jax.experimental.pallas import tpu as pltpu

H_A = 4
HD = 64
DK = 2 * HD
W_A = H_A * DK
H_B = 4
CB = 128
W_B = H_B * CB
CHUNK = 128
N_BUCKETS = 32
MAX_DISTANCE = 128
LN_EPS = 1e-5
SCALE = HD ** -0.5
LOG2E = math.log2(math.e)
NEG = -1e30
MASKED_BUCKET = N_BUCKETS

TM = 512
TQ = 512
TK = 512
QH = TQ // 2
ONES_ROWS = 16
VA = DK + ONES_ROWS
T_PAD = 8
PAGES_PER_GROUP = 4
RING_GROUPS = 4
VMEM_LIMIT = 56 * 1024 * 1024

_F32 = jnp.float32
_BF16 = jnp.bfloat16
_NT = (((1,), (1,)), ((), ()))


def _bucket_np(dist):
    n = np.maximum(dist, 0)
    max_exact = N_BUCKETS // 2
    nf = np.maximum(n, 1).astype(np.float32)
    large = max_exact + (np.log(nf / max_exact) / math.log(MAX_DISTANCE / max_exact)
                         * (N_BUCKETS - max_exact)).astype(np.int32)
    large = np.minimum(large, N_BUCKETS - 1)
    return np.where(n < max_exact, n, large).astype(np.int32)


def _silu(x):
    return x * jax.nn.sigmoid(x)


def _layer_norm_rows(x, g, b):
    mu = jnp.mean(x, axis=-1, keepdims=True)
    xc = x - mu
    var = jnp.mean(xc * xc, axis=-1, keepdims=True)
    return xc * lax.rsqrt(var + LN_EPS) * g + b


def _rms_norm_rows(x, g):
    return x * lax.rsqrt(jnp.mean(x * x, axis=-1, keepdims=True) + LN_EPS) * g


def _bias_kernel(tab_ref, idx_ref, o_ref):
    h = pl.program_id(0)
    idx = idx_ref[0]
    far = tab_ref[N_BUCKETS - 1, h]
    acc = jnp.full(idx.shape, NEG, _F32)
    for b in range(N_BUCKETS):
        acc = jnp.where(idx == b, (tab_ref[b, h] - far) * LOG2E, acc)
    o_ref[0, 0] = acc


def _bias_tiles(table, idx):
    n, r, c = idx.shape
    return pl.pallas_call(
        _bias_kernel,
        grid=(H_A, n),
        in_specs=[pl.BlockSpec(memory_space=pltpu.SMEM),
                  pl.BlockSpec((1, r, c), lambda h, i: (i, 0, 0))],
        out_specs=pl.BlockSpec((1, 1, r, c), lambda h, i: (h, i, 0, 0)),
        out_shape=jax.ShapeDtypeStruct((H_A, n, r, c), _F32),
        name="bias_tiles",
    )(table, idx)


def _prompt_proj_kernel(x_ref, w_ref, lng_ref, lnb_ref, wt_ref, bs_ref,
                        kf_ref, vf_ref, kb_ref, qt_ref, vt_ref, ga_ref, g_ref):
    xb = x_ref[0].astype(_BF16)
    hm = jnp.dot(xb, w_ref[...], preferred_element_type=_F32)
    k = hm[:, W_A:2 * W_A]
    v = hm[:, 2 * W_A:3 * W_A]
    for h in range(H_A):
        cols = slice(h * DK, (h + 1) * DK)
        kf_ref[0, pl.ds(h, TM, stride=H_A), :] = k[:, cols]
        vf_ref[0, pl.ds(h, TM, stride=H_A), :] = v[:, cols]
    kb_ref[0] = k.astype(_BF16)
    ga_ref[0] = hm[:, 3 * W_A:4 * W_A]
    qt_ref[0] = hm[:, 0:W_A].T.astype(_BF16)
    vt = v.T.astype(_BF16)
    for h in range(H_A):
        vt_ref[0, h, 0:DK, :] = vt[h * DK:(h + 1) * DK]
        vt_ref[0, h, DK:VA, :] = jnp.ones((ONES_ROWS, TM), _BF16)
    o = 4 * W_A
    u = hm[:, o:o + W_B]
    vb = hm[:, o + W_B:o + 2 * W_B]
    gb = hm[:, o + 2 * W_B:o + 3 * W_B]
    vn = _layer_norm_rows(vb, lng_ref[...], lnb_ref[...]).astype(_BF16)
    gate = u * _silu(gb)
    for c in range(TM // CHUNK):
        rows = slice(c * CHUNK, (c + 1) * CHUNK)
        for h in range(H_B):
            cols = slice(h * CB, (h + 1) * CB)
            s = jnp.dot(wt_ref[h], vn[rows, cols], preferred_element_type=_F32) + bs_ref[h]
            g_ref[0, rows, cols] = (gate[rows, cols] * s).astype(_BF16)


def _prompt_proj(x, w_all, ln_g, ln_b, wt, bsb):
    b, s, d = x.shape
    row = lambda bi, si: (bi, si, 0)
    col = lambda bi, si: (bi, 0, si)
    const2 = lambda bi, si: (0, 0)
    const3 = lambda bi, si: (0, 0, 0)
    f32_heads = jax.ShapeDtypeStruct((b, s * H_A, DK), _F32)
    bf_rows = jax.ShapeDtypeStruct((b, s, W_A), _BF16)
    bf_cols = jax.ShapeDtypeStruct((b, W_A, s), _BF16)
    return pl.pallas_call(
        _prompt_proj_kernel,
        grid=(b, s // TM),
        in_specs=[pl.BlockSpec((1, TM, d), row),
                  pl.BlockSpec(w_all.shape, const2),
                  pl.BlockSpec(ln_g.shape, const2),
                  pl.BlockSpec(ln_b.shape, const2),
                  pl.BlockSpec(wt.shape, const3),
                  pl.BlockSpec(bsb.shape, const3)],
        out_specs=[pl.BlockSpec((1, TM * H_A, DK), row), pl.BlockSpec((1, TM * H_A, DK), row),
                   pl.BlockSpec((1, TM, W_A), row),
                   pl.BlockSpec((1, W_A, TM), col),
                   pl.BlockSpec((1, H_A, VA, TM), lambda bi, si: (bi, 0, 0, si)),
                   pl.BlockSpec((1, TM, W_A), row), pl.BlockSpec((1, TM, W_B), row)],
        out_shape=[f32_heads, f32_heads, bf_rows, bf_cols,
                   jax.ShapeDtypeStruct((b, H_A, VA, s), _BF16),
                   jax.ShapeDtypeStruct((b, s, W_A), _F32), jax.ShapeDtypeStruct((b, s, W_B), _BF16)],
        compiler_params=pltpu.CompilerParams(dimension_semantics=("arbitrary", "arbitrary"),
                                             vmem_limit_bytes=VMEM_LIMIT),
        name="prompt_proj",
    )(x, w_all, ln_g, ln_b, wt, bsb)


def _out_proj_kernel(a_ref, g_ref, x_ref, wa_ref, wg_ref, lng_ref, lnb_ref, y_ref, *, alpha):
    out = (jnp.dot(a_ref[...], wa_ref[...], preferred_element_type=_F32)
           + jnp.dot(g_ref[...], wg_ref[...], preferred_element_type=_F32))
    z = alpha * x_ref[...] + out
    y_ref[...] = _layer_norm_rows(z, lng_ref[...], lnb_ref[...])


def _out_proj(a, g, x, wo_a, wo_g, ln_g, ln_b, alpha):
    n, d = x.shape
    tm = min(n, TM)
    row = lambda i: (i, 0)
    const = lambda i: (0, 0)
    return pl.pallas_call(
        functools.partial(_out_proj_kernel, alpha=alpha),
        grid=(n // tm,),
        in_specs=[pl.BlockSpec((tm, W_A), row), pl.BlockSpec((tm, W_B), row), pl.BlockSpec((tm, d), row),
                  pl.BlockSpec(wo_a.shape, const), pl.BlockSpec(wo_g.shape, const),
                  pl.BlockSpec(ln_g.shape, const), pl.BlockSpec(ln_b.shape, const)],
        out_specs=pl.BlockSpec((tm, d), row),
        out_shape=jax.ShapeDtypeStruct((n, d), _F32),
        compiler_params=pltpu.CompilerParams(dimension_semantics=("arbitrary",),
                                             vmem_limit_bytes=VMEM_LIMIT),
        name="out_proj",
    )(a, g, x, wo_a, wo_g, ln_g, ln_b)


def _sample_proj_kernel(x_ref, w_ref, lng_ref, lnb_ref, cs_ref, bs_ref, lp_ref,
                        q_ref, k_ref, v_ref, ga_ref, g_ref, vn_ref, lam_ref, *, lam_init, t_valid):
    rows = x_ref.shape[0]
    h = jnp.dot(x_ref[...].astype(_BF16), w_ref[...], preferred_element_type=_F32)
    q_ref[...] = h[:, 0:W_A]
    k_ref[...] = h[:, W_A:2 * W_A]
    v_ref[...] = h[:, 2 * W_A:3 * W_A]
    ga_ref[...] = h[:, 3 * W_A:4 * W_A]
    o = 4 * W_A
    u = h[:, o:o + W_B]
    vb = h[:, o + W_B:o + 2 * W_B]
    gb = h[:, o + 2 * W_B:o + 3 * W_B]
    vn = _layer_norm_rows(vb, lng_ref[...], lnb_ref[...])
    vn_ref[...] = vn
    vn3 = vn.reshape(rows // T_PAD, T_PAD, W_B)
    s3 = jnp.broadcast_to(bs_ref[...][None], vn3.shape)
    for j in range(t_valid):
        s3 = s3 + vn3[:, j:j + 1, :] * cs_ref[j][None]
    g_ref[...] = (u * s3.reshape(rows, W_B) * _silu(gb)).astype(_BF16)
    lp = lp_ref[...]
    e1 = jnp.exp(jnp.sum(lp[0:1] * lp[1:2], axis=-1, keepdims=True))
    e2 = jnp.exp(jnp.sum(lp[2:3] * lp[3:4], axis=-1, keepdims=True))
    lam_ref[...] = jnp.broadcast_to(e1 - e2 + lam_init, lam_ref.shape)


def _sample_proj(x, w, ln_g, ln_b, cs, bs8, lam_params, lam_init, t_valid):
    n, d = x.shape
    tm = min(n, 256)
    row = lambda i: (i, 0)
    const2 = lambda i: (0, 0)
    const3 = lambda i: (0, 0, 0)
    f32_rows = jax.ShapeDtypeStruct((n, W_A), _F32)
    return pl.pallas_call(
        functools.partial(_sample_proj_kernel, lam_init=lam_init, t_valid=t_valid),
        grid=(n // tm,),
        in_specs=[pl.BlockSpec((tm, d), row), pl.BlockSpec(w.shape, const2),
                  pl.BlockSpec(ln_g.shape, const2), pl.BlockSpec(ln_b.shape, const2),
                  pl.BlockSpec(cs.shape, const3), pl.BlockSpec(bs8.shape, const2),
                  pl.BlockSpec(lam_params.shape, const2)],
        out_specs=[pl.BlockSpec((tm, W_A), row)] * 4
                  + [pl.BlockSpec((tm, W_B), row), pl.BlockSpec((tm, W_B), row),
                     pl.BlockSpec((8, 128), const2)],
        out_shape=[f32_rows] * 4 + [jax.ShapeDtypeStruct((n, W_B), _BF16),
                                    jax.ShapeDtypeStruct((n, W_B), _F32),
                                    jax.ShapeDtypeStruct((8, 128), _F32)],
        compiler_params=pltpu.CompilerParams(dimension_semantics=("arbitrary",),
                                             vmem_limit_bytes=VMEM_LIMIT),
        name="sample_proj",
    )(x, w, ln_g, ln_b, cs, bs8, lam_params)


def _attn_kernel(pt_ref, lam_ref,
                 qt_ref, k_ref, vt_ref, bias_ref, ga_ref, sg_ref,
                 q8_ref, kn_ref, vn_ref, ga8_ref, hm_ref, neg_ref, bl_ref, bn_ref, ck_ref, cv_ref,
                 a_ref, a8_ref,
                 qz_ref, sa_ref, sb_ref, m_ref, l_ref, acc_ref,
                 kbuf, vbuf, ksem, vsem, qm_ref, ss_ref, ms_ref, ls_ref, accs_ref, cnt_ref,
                 *, lam_init, n_groups, n_seq):
    gp = PAGES_PER_GROUP
    n_rows = 2 * H_A * T_PAD
    page_rows = kbuf.shape[1]
    qi = pl.program_id(2)
    first_step = jnp.logical_and(jnp.logical_and(pl.program_id(0) == 0, pl.program_id(1) == 0), qi == 0)

    def group_copies(hbm_ref, buf, sem, seq, grp):
        slot = jnp.bitwise_and(grp, RING_GROUPS - 1)
        return [pltpu.make_async_copy(hbm_ref.at[pt_ref[seq, grp * gp + i]], buf.at[slot * gp + i], sem.at[slot])
                for i in range(gp)]

    @pl.when(first_step)
    def _():
        cnt_ref[0] = 0
        cnt_ref[1] = 0
        for g in range(RING_GROUPS):
            for c in group_copies(ck_ref, kbuf, ksem, 0, g) + group_copies(cv_ref, vbuf, vsem, 0, g):
                c.start()

    def sample_pre():
        seq = cnt_ref[0]
        grp = cnt_ref[1]
        active = seq < n_seq

        @pl.when(jnp.logical_and(active, grp == 0))
        def _():
            q8 = q8_ref[seq]
            per_head = [q8[:, h * DK:(h + 1) * DK] for h in range(H_A)]
            qm_ref[...] = jnp.concatenate(per_head + per_head, axis=0) * hm_ref[...]
            ms_ref[...] = jnp.full(ms_ref.shape, NEG, _F32)
            ls_ref[...] = jnp.zeros(ls_ref.shape, _F32)
            accs_ref[...] = jnp.zeros(accs_ref.shape, _F32)

        @pl.when(active)
        def _():
            for c in group_copies(ck_ref, kbuf, ksem, seq, grp) + group_copies(cv_ref, vbuf, vsem, seq, grp):
                c.wait()

    def sample_scores():
        grp = cnt_ref[1]
        slot = jnp.bitwise_and(grp, RING_GROUPS - 1)
        last_group = (grp == n_groups - 1).astype(_F32)
        qm = qm_ref[...]
        for i in range(gp):
            s = lax.dot_general(qm, kbuf[slot * gp + i], _NT, preferred_element_type=_F32) + neg_ref[...]
            if i == gp - 1:
                s = s + last_group * bl_ref[...]
            ss_ref[:, i * page_rows:(i + 1) * page_rows] = s

    def sample_consume():
        slot = jnp.bitwise_and(cnt_ref[1], RING_GROUPS - 1)
        smax = ss_ref[:, 0:page_rows]
        for i in range(1, gp):
            smax = jnp.maximum(smax, ss_ref[:, i * page_rows:(i + 1) * page_rows])
        m_old = ms_ref[...]
        m_new = jnp.maximum(m_old, jnp.max(smax, axis=-1, keepdims=True))
        alpha = jnp.exp2(m_old - m_new)
        l = alpha * ls_ref[...]
        acc = alpha * accs_ref[...]
        for i in range(gp):
            p = jnp.exp2(ss_ref[:, i * page_rows:(i + 1) * page_rows] - m_new)
            l = l + p
            acc = acc + jnp.dot(p, vbuf[slot * gp + i], preferred_element_type=_F32)
        ls_ref[...] = l
        accs_ref[...] = acc
        ms_ref[...] = m_new

    def sample_post():
        seq = cnt_ref[0]
        grp = cnt_ref[1]
        active = seq < n_seq

        @pl.when(active)
        def _():
            ahead = grp + RING_GROUPS
            wraps = (ahead >= n_groups).astype(jnp.int32)
            seq2 = seq + wraps
            grp2 = ahead - wraps * n_groups

            @pl.when(seq2 < n_seq)
            def _():
                for c in (group_copies(ck_ref, kbuf, ksem, seq2, grp2)
                          + group_copies(cv_ref, vbuf, vsem, seq2, grp2)):
                    c.start()

        @pl.when(jnp.logical_and(active, grp == n_groups - 1))
        def _():
            sn = (lax.dot_general(qm_ref[...], kn_ref[seq], _NT, preferred_element_type=_F32)
                  + bn_ref[...])
            m_o = ms_ref[...]
            m_f = jnp.maximum(m_o, jnp.max(sn, axis=-1, keepdims=True))
            al = jnp.exp2(m_o - m_f)
            pn = jnp.exp2(sn - m_f)
            l_tot = al * jnp.sum(ls_ref[...], axis=-1, keepdims=True) + jnp.sum(pn, axis=-1, keepdims=True)
            acc = al * accs_ref[...] + jnp.dot(pn, vn_ref[seq], preferred_element_type=_F32)
            o_full = acc * (1.0 / l_tot)
            half = n_rows // 2
            od = o_full[0:half] - lam_ref[0, 0] * o_full[half:n_rows]
            ga = ga8_ref[seq]
            out = []
            for h in range(H_A):
                oh = _rms_norm_rows(od[h * T_PAD:(h + 1) * T_PAD], sg_ref[...]) * (1.0 - lam_init)
                out.append((oh * _silu(ga[:, h * DK:(h + 1) * DK])).astype(_BF16))
            a8_ref[seq] = jnp.concatenate(out, axis=1)

        nxt = grp + 1
        wrap = (nxt == n_groups).astype(jnp.int32)
        cnt_ref[1] = nxt - wrap * n_groups
        cnt_ref[0] = seq + wrap

    qt = qt_ref[0]
    row = lax.broadcasted_iota(jnp.int32, qt.shape, 0)
    zero = jnp.zeros_like(qt)
    qz_ref[:, 0:TQ] = jnp.where(row < HD, qt, zero)
    qz_ref[:, TQ:2 * TQ] = jnp.where(row >= HD, qt, zero)
    m_ref[...] = jnp.full(m_ref.shape, NEG, _F32)
    l_ref[...] = jnp.zeros(l_ref.shape, _F32)
    acc_ref[...] = jnp.zeros(acc_ref.shape, _F32)

    def scores(j, dst_ref):
        start = pl.multiple_of(j * TK, TK)
        dst_ref[...] = jnp.dot(k_ref[0, pl.ds(start, TK), :], qz_ref[...], preferred_element_type=_F32)

    def consume(src_ref, j, bias_idx):
        start = pl.multiple_of(j * TK, TK)
        vblk = vt_ref[0, 0, :, pl.ds(start, TK)]
        for mp in range(2):
            for hf in range(TQ // QH):
                qs = slice(hf * QH, (hf + 1) * QH)
                sm = src_ref[:, mp * TQ + hf * QH:mp * TQ + (hf + 1) * QH]
                if bias_idx is not None:
                    sm = sm + bias_ref[0, bias_idx, :, qs]
                m_old = m_ref[mp, :, qs]
                m_new = jnp.maximum(m_old, jnp.max(sm, axis=0, keepdims=True))
                alpha = jnp.exp2(m_old - m_new)
                p = jnp.exp2(sm - m_new).astype(_BF16)
                pv = jnp.dot(vblk, p, preferred_element_type=_F32)
                acc_ref[mp, :, qs] = acc_ref[mp, :, qs] * alpha + pv[0:DK]
                l_ref[mp, :, qs] = l_ref[mp, :, qs] * alpha + pv[DK:DK + 1]
                m_ref[mp, :, qs] = m_new

    def tile(next_j, next_ref, cur_ref, cur_j, bias_idx):
        sample_pre()
        sample_scores()
        if next_ref is not None:
            scores(next_j, next_ref)
        sample_consume()
        consume(cur_ref, cur_j, bias_idx)
        sample_post()

    n_far = jnp.maximum(qi - 1, 0)
    odd = jnp.bitwise_and(n_far, 1)
    first_in_b = jnp.logical_or(qi == 0, odd == 1)

    @pl.when(first_in_b)
    def _():
        scores(0, sb_ref)

    @pl.when(jnp.logical_not(first_in_b))
    def _():
        scores(0, sa_ref)

    @pl.when(jnp.logical_and(qi >= 1, odd == 1))
    def _():
        tile(1, sa_ref, sb_ref, 0, None)

    def far_pair(i, carry):
        t0 = odd + 2 * i
        tile(t0 + 1, sb_ref, sa_ref, t0, None)
        tile(t0 + 2, sa_ref, sb_ref, t0 + 1, None)
        return carry

    lax.fori_loop(0, lax.shift_right_logical(n_far, 1), far_pair, 0)

    @pl.when(qi >= 1)
    def _():
        tile(qi, sb_ref, sa_ref, qi - 1, 0)

    tile(None, None, sb_ref, qi, 1)

    lam = lam_ref[0, 0]
    o_t = acc_ref[0] * (1.0 / l_ref[0]) - lam * (acc_ref[1] * (1.0 / l_ref[1]))
    o = o_t.T
    o = _rms_norm_rows(o, sg_ref[...]) * (1.0 - lam_init)
    a_ref[0] = (o * _silu(ga_ref[0])).astype(_BF16)


def _attn(page_table, lam, qt, kb, vt, bias, ga, subln_g, q8, kn, vn, ga8, halfmask, negmask, bias_last, bias_new,
          ck, cv, lam_init):
    b, s, _ = kb.shape
    db, n_pages = page_table.shape
    gp = PAGES_PER_GROUP
    n_groups = n_pages // gp
    n_q = s // TQ
    assert n_pages % gp == 0 and n_groups % RING_GROUPS == 0
    assert b * H_A * (n_q * (n_q + 1) // 2) >= db * n_groups
    page_rows = ck.shape[1]
    n_rows = 2 * H_A * T_PAD
    whole = lambda shape: pl.BlockSpec(shape, lambda bi, h, qi, pt: (0,) * len(shape))
    grid_spec = pltpu.PrefetchScalarGridSpec(
        num_scalar_prefetch=1,
        grid=(b, H_A, n_q),
        in_specs=[pl.BlockSpec(memory_space=pltpu.SMEM),
                  pl.BlockSpec((1, DK, TQ), lambda bi, h, qi, pt: (bi, h, qi)),
                  pl.BlockSpec((1, s, DK), lambda bi, h, qi, pt: (bi, 0, h)),
                  pl.BlockSpec((1, 1, VA, s), lambda bi, h, qi, pt: (bi, h, 0, 0)),
                  pl.BlockSpec((1, 2, TK, TQ), lambda bi, h, qi, pt: (h, 0, 0, 0)),
                  pl.BlockSpec((1, TQ, DK), lambda bi, h, qi, pt: (bi, qi, h)),
                  whole(subln_g.shape),
                  whole(q8.shape), whole(kn.shape), whole(vn.shape), whole(ga8.shape),
                  whole(halfmask.shape), whole(negmask.shape), whole(bias_last.shape), whole(bias_new.shape),
                  pl.BlockSpec(memory_space=pl.ANY), pl.BlockSpec(memory_space=pl.ANY)],
        out_specs=[pl.BlockSpec((1, TQ, DK), lambda bi, h, qi, pt: (bi, qi, h)),
                   whole((db, T_PAD, W_A))],
        scratch_shapes=[pltpu.VMEM((DK, 2 * TQ), _BF16),
                        pltpu.VMEM((TK, 2 * TQ), _F32),
                        pltpu.VMEM((TK, 2 * TQ), _F32),
                        pltpu.VMEM((2, 1, TQ), _F32),
                        pltpu.VMEM((2, 1, TQ), _F32),
                        pltpu.VMEM((2, DK, TQ), _F32),
                        pltpu.VMEM((RING_GROUPS * gp, page_rows, DK), _F32),
                        pltpu.VMEM((RING_GROUPS * gp, page_rows, DK), _F32),
                        pltpu.SemaphoreType.DMA((RING_GROUPS,)),
                        pltpu.SemaphoreType.DMA((RING_GROUPS,)),
                        pltpu.VMEM((n_rows, DK), _F32),
                        pltpu.VMEM((n_rows, gp * page_rows), _F32),
                        pltpu.VMEM((n_rows, 1), _F32),
                        pltpu.VMEM((n_rows, page_rows), _F32),
                        pltpu.VMEM((n_rows, DK), _F32),
                        pltpu.SMEM((2,), jnp.int32)],
    )
    return pl.pallas_call(
        functools.partial(_attn_kernel, lam_init=lam_init, n_groups=n_groups, n_seq=db),
        grid_spec=grid_spec,
        out_shape=[jax.ShapeDtypeStruct((b, s, W_A), _BF16), jax.ShapeDtypeStruct((db, T_PAD, W_A), _BF16)],
        compiler_params=pltpu.CompilerParams(dimension_semantics=("arbitrary", "arbitrary", "arbitrary"),
                                             vmem_limit_bytes=VMEM_LIMIT),
        name="attn",
    )(page_table, lam, qt, kb, vt, bias, ga, subln_g, q8, kn, vn, ga8, halfmask, negmask, bias_last, bias_new,
      ck, cv)


def kernel(x_prompt, x_sample, cache_k, cache_v, page_table, w_in, w_out, lambda_q1, lambda_k1, lambda_q2,
           lambda_k2, subln_g, rel_bias, sgu_ln_g, sgu_ln_b, sgu_w, sgu_b, post_ln_g, post_ln_b):
    depth = w_in.shape[0]
    assert depth == 1, "single trunk layer"
    b, s, d = x_prompt.shape
    db, t, _ = x_sample.shape
    n_pool, page = cache_k.shape[1], cache_k.shape[2]
    assert s % TQ == 0 and s % TM == 0 and TQ == TK and TQ >= MAX_DISTANCE
    assert t <= T_PAD and page >= MAX_DISTANCE
    lam_init = 0.8 - 0.6 * math.exp(-0.3 * 0)
    alpha = (2.0 * depth) ** 0.25

    col_scale = jnp.where(jnp.arange(w_in.shape[2]) < W_A, SCALE * LOG2E, 1.0).astype(_F32)
    w_all = (w_in[0] * col_scale[None, :]).astype(_BF16)
    wo_a = w_out[0, 0:W_A].astype(_BF16)
    wo_g = w_out[0, W_A:].astype(_BF16)
    ln_g = sgu_ln_g[0][None]
    ln_b = sgu_ln_b[0][None]
    pg_ = post_ln_g[0][None]
    pb_ = post_ln_b[0][None]
    sg = subln_g[0][None]
    tril = jnp.tril(jnp.ones((CHUNK, CHUNK), _F32))
    wt = (sgu_w[0] * tril).astype(_BF16)
    bsb = jnp.broadcast_to(sgu_b[0][:, :, None], (H_B, CHUNK, CB))
    wsmall = sgu_w[0][:, :T_PAD, :T_PAD] * tril[:T_PAD, :T_PAD]
    valid_t = (jnp.arange(T_PAD) < t).astype(_F32)
    cs = jnp.transpose(wsmall, (2, 1, 0)) * valid_t[None, :, None]
    cs = jnp.repeat(cs, CB, axis=2)
    bs8 = jnp.repeat(jnp.transpose(sgu_b[0][:, :T_PAD]) * valid_t[:, None], CB, axis=1)
    lam_params = jnp.concatenate([lambda_q1, lambda_k1, lambda_q2, lambda_k2], axis=0)

    kk = np.arange(TK)[:, None]
    qq = np.arange(TQ)[None, :]
    idx_sub = _bucket_np(TQ + qq - kk)
    idx_diag = np.where(qq >= kk, _bucket_np(qq - kk), MASKED_BUCKET)
    bias_prompt = _bias_tiles(rel_bias, jnp.asarray(np.stack([idx_sub, idx_diag]).astype(np.int32)))
    tt = np.arange(T_PAD)[:, None]
    ii = np.arange(page * H_A)[None, :] // H_A
    idx_last = _bucket_np(np.where(tt < t, page + tt - ii, MAX_DISTANCE))
    idx_new = np.where((ii <= tt) & (tt < t), _bucket_np(tt - ii), MASKED_BUCKET)
    bias_sample = _bias_tiles(rel_bias, jnp.asarray(np.stack([idx_last, idx_new]).astype(np.int32)))
    r = np.arange(2 * H_A * T_PAD)
    r_head = (r // T_PAD) % H_A
    r_map = r // (H_A * T_PAD)
    lane_head = np.arange(page * H_A) % H_A
    own_head = r_head[:, None] == lane_head[None, :]
    negmask = jnp.asarray(np.where(own_head, 0.0, NEG).astype(np.float32))
    halfmask = jnp.asarray((np.arange(DK)[None, :] // HD == r_map[:, None]).astype(np.float32))
    bias_last = jnp.tile(bias_sample[:, 0].reshape(H_A * T_PAD, page * H_A), (2, 1))
    bias_new = (jnp.tile(bias_sample[:, 1, :, :T_PAD * H_A].reshape(H_A * T_PAD, T_PAD * H_A), (2, 1))
                + negmask[:, :T_PAD * H_A])

    xs = jnp.pad(x_sample, ((0, 0), (0, T_PAD - t), (0, 0))).reshape(db * T_PAD, d)
    q8, k8, v8, ga8, g8, vn8, lam_tile = _sample_proj(xs, w_all, ln_g, ln_b, cs, bs8, lam_params, lam_init, t)
    lam = lam_tile[0:1, 0:1]
    kf, vf, kb, qt, vt, ga, g = _prompt_proj(x_prompt, w_all, ln_g, ln_b, wt, bsb)

    ck = cache_k[0].reshape(n_pool, page * H_A, DK)
    cv = cache_v[0].reshape(n_pool, page * H_A, DK)
    a, a8 = _attn(page_table, lam, qt, kb, vt, bias_prompt, ga, sg,
                  q8.reshape(db, T_PAD, W_A), k8.reshape(db, T_PAD * H_A, DK), v8.reshape(db, T_PAD * H_A, DK),
                  ga8.reshape(db, T_PAD, W_A), halfmask, negmask, bias_last, bias_new, ck, cv, lam_init)

    ys = _out_proj(a8.reshape(db * T_PAD, W_A), g8, xs, wo_a, wo_g, pg_, pb_, alpha)
    yp = _out_proj(a.reshape(b * s, W_A), g.reshape(b * s, W_B), x_prompt.reshape(b * s, d),
                   wo_a, wo_g, pg_, pb_, alpha)
    y_sample = ys.reshape(db, T_PAD, d)[:, :t]
    y_prompt = yp.reshape(b, s, d)
    nks = k8.reshape(db, T_PAD, H_A, DK)[None, :, :t]
    nvs = v8.reshape(db, T_PAD, H_A, DK)[None, :, :t]
    nsv = vn8.reshape(db, T_PAD, H_B, CB)[None, :, :t]
    nkp = kf.reshape(1, b, s, H_A, DK)
    nvp = vf.reshape(1, b, s, H_A, DK)
    return (y_prompt, y_sample, nkp, nvp, nks, nvs, nsv)
```

```python
import functools
import math

import numpy as np
import jax
import jax.numpy as jnp
from jax import lax
from jax.experimental import pallas as pl
from jax.experimental.pallas import tpu as pltpu

H_A = 4
HD = 64
DK = 2 * HD
W_A = H_A * DK
H_B = 4
CB = 128
W_B = H_B * CB
CHUNK = 128
N_BUCKETS = 32
MAX_DISTANCE = 128
LN_EPS = 1e-5
SCALE = HD ** -0.5
LOG2E = math.log2(math.e)
NEG = -1e30
MASKED_BUCKET = N_BUCKETS

TM = 512
TQ = 512
TK = 512
QH = TQ // 2
ONES_ROWS = 16
VA = DK + ONES_ROWS
T_PAD = 8
PAGES_PER_GROUP = 4
RING_GROUPS = 4
VMEM_LIMIT = 56 * 1024 * 1024

_F32 = jnp.float32
_BF16 = jnp.bfloat16
_NT = (((1,), (1,)), ((), ()))


def _bucket_np(dist):
    n = np.maximum(dist, 0)
    max_exact = N_BUCKETS // 2
    nf = np.maximum(n, 1).astype(np.float32)
    large = max_exact + (np.log(nf / max_exact) / math.log(MAX_DISTANCE / max_exact)
                         * (N_BUCKETS - max_exact)).astype(np.int32)
    large = np.minimum(large, N_BUCKETS - 1)
    return np.where(n < max_exact, n, large).astype(np.int32)


def _silu(x):
    return x * jax.nn.sigmoid(x)


def _layer_norm_rows(x, g, b):
    mu = jnp.mean(x, axis=-1, keepdims=True)
    xc = x - mu
    var = jnp.mean(xc * xc, axis=-1, keepdims=True)
    return xc * lax.rsqrt(var + LN_EPS) * g + b


def _rms_norm_rows(x, g):
    return x * lax.rsqrt(jnp.mean(x * x, axis=-1, keepdims=True) + LN_EPS) * g


def _bias_kernel(tab_ref, idx_ref, o_ref):
    h = pl.program_id(0)
    idx = idx_ref[0]
    far = tab_ref[N_BUCKETS - 1, h]
    acc = jnp.full(idx.shape, NEG, _F32)
    for b in range(N_BUCKETS):
        acc = jnp.where(idx == b, (tab_ref[b, h] - far) * LOG2E, acc)
    o_ref[0, 0] = acc


def _bias_tiles(table, idx):
    n, r, c = idx.shape
    return pl.pallas_call(
        _bias_kernel,
        grid=(H_A, n),
        in_specs=[pl.BlockSpec(memory_space=pltpu.SMEM),
                  pl.BlockSpec((1, r, c), lambda h, i: (i, 0, 0))],
        out_specs=pl.BlockSpec((1, 1, r, c), lambda h, i: (h, i, 0, 0)),
        out_shape=jax.ShapeDtypeStruct((H_A, n, r, c), _F32),
        name="bias_tiles",
    )(table, idx)


def _prompt_proj_kernel(x_ref, w_ref, lng_ref, lnb_ref, wt_ref, bs_ref,
                        kf_ref, vf_ref, kb_ref, qt_ref, vt_ref, ga_ref, g_ref):
    xb = x_ref[0].astype(_BF16)
    hm = jnp.dot(xb, w_ref[...], preferred_element_type=_F32)
    k = hm[:, W_A:2 * W_A]
    v = hm[:, 2 * W_A:3 * W_A]
    for h in range(H_A):
        cols = slice(h * DK, (h + 1) * DK)
        kf_ref[0, pl.ds(h, TM, stride=H_A), :] = k[:, cols]
        vf_ref[0, pl.ds(h, TM, stride=H_A), :] = v[:, cols]
    kb_ref[0] = k.astype(_BF16)
    ga_ref[0] = hm[:, 3 * W_A:4 * W_A]
    qt_ref[0] = hm[:, 0:W_A].T.astype(_BF16)
    vt = v.T.astype(_BF16)
    for h in range(H_A):
        vt_ref[0, h, 0:DK, :] = vt[h * DK:(h + 1) * DK]
        vt_ref[0, h, DK:VA, :] = jnp.ones((ONES_ROWS, TM), _BF16)
    o = 4 * W_A
    u = hm[:, o:o + W_B]
    vb = hm[:, o + W_B:o + 2 * W_B]
    gb = hm[:, o + 2 * W_B:o + 3 * W_B]
    vn = _layer_norm_rows(vb, lng_ref[...], lnb_ref[...]).astype(_BF16)
    gate = u * _silu(gb)
    for c in range(TM // CHUNK):
        rows = slice(c * CHUNK, (c + 1) * CHUNK)
        for h in range(H_B):
            cols = slice(h * CB, (h + 1) * CB)
            s = jnp.dot(wt_ref[h], vn[rows, cols], preferred_element_type=_F32) + bs_ref[h]
            g_ref[0, rows, cols] = (gate[rows, cols] * s).astype(_BF16)


def _prompt_proj(x, w_all, ln_g, ln_b, wt, bsb):
    b, s, d = x.shape
    row = lambda bi, si: (bi, si, 0)
    col = lambda bi, si: (bi, 0, si)
    const2 = lambda bi, si: (0, 0)
    const3 = lambda bi, si: (0, 0, 0)
    f32_heads = jax.ShapeDtypeStruct((b, s * H_A, DK), _F32)
    bf_rows = jax.ShapeDtypeStruct((b, s, W_A), _BF16)
    bf_cols = jax.ShapeDtypeStruct((b, W_A, s), _BF16)
    return pl.pallas_call(
        _prompt_proj_kernel,
        grid=(b, s // TM),
        in_specs=[pl.BlockSpec((1, TM, d), row),
                  pl.BlockSpec(w_all.shape, const2),
                  pl.BlockSpec(ln_g.shape, const2),
                  pl.BlockSpec(ln_b.shape, const2),
                  pl.BlockSpec(wt.shape, const3),
                  pl.BlockSpec(bsb.shape, const3)],
        out_specs=[pl.BlockSpec((1, TM * H_A, DK), row), pl.BlockSpec((1, TM * H_A, DK), row),
                   pl.BlockSpec((1, TM, W_A), row),
                   pl.BlockSpec((1, W_A, TM), col),
                   pl.BlockSpec((1, H_A, VA, TM), lambda bi, si: (bi, 0, 0, si)),
                   pl.BlockSpec((1, TM, W_A), row), pl.BlockSpec((1, TM, W_B), row)],
        out_shape=[f32_heads, f32_heads, bf_rows, bf_cols,
                   jax.ShapeDtypeStruct((b, H_A, VA, s), _BF16),
                   jax.ShapeDtypeStruct((b, s, W_A), _F32), jax.ShapeDtypeStruct((b, s, W_B), _BF16)],
        compiler_params=pltpu.CompilerParams(dimension_semantics=("arbitrary", "arbitrary"),
                                             vmem_limit_bytes=VMEM_LIMIT),
        name="prompt_proj",
    )(x, w_all, ln_g, ln_b, wt, bsb)


def _prompt_attn_kernel(lam_ref, qt_ref, k_ref, vt_ref, bias_ref, ga_ref, sg_ref, a_ref,
                        qz_ref, qzn_ref, sa_ref, sb_ref, sc_ref, mxa_ref, mxb_ref, mxc_ref, m_ref, l_ref, acc_ref,
                        *, lam_init):
    qi = pl.program_id(2)
    n_q = pl.num_programs(2)

    def split_maps(qt, dst_ref):
        row = lax.broadcasted_iota(jnp.int32, qt.shape, 0)
        zero = jnp.zeros_like(qt)
        dst_ref[:, 0:TQ] = jnp.where(row < HD, qt, zero)
        dst_ref[:, TQ:2 * TQ] = jnp.where(row >= HD, qt, zero)

    q_start = pl.multiple_of(qi * TQ, TQ)
    split_maps(qt_ref[0, :, pl.ds(q_start, TQ)], qz_ref)
    m_ref[...] = jnp.full(m_ref.shape, NEG, _F32)
    l_ref[...] = jnp.zeros(l_ref.shape, _F32)
    acc_ref[...] = jnp.zeros(acc_ref.shape, _F32)

    def scores(j, dst_ref, mx_ref, bias_idx=None, q_ref=qz_ref):
        start = pl.multiple_of(j * TK, TK)
        s = jnp.dot(k_ref[0, pl.ds(start, TK), :], q_ref[...], preferred_element_type=_F32)
        if bias_idx is not None:
            bias = bias_ref[0, bias_idx]
            s = jnp.concatenate([s[:, 0:TQ] + bias, s[:, TQ:2 * TQ] + bias], axis=1)
        dst_ref[...] = s
        mx_ref[...] = jnp.max(s, axis=0, keepdims=True)

    def consume(src_ref, mx_ref, j):
        start = pl.multiple_of(j * TK, TK)
        vblk = vt_ref[0, 0, :, pl.ds(start, TK)]
        for mp in range(2):
            for hf in range(TQ // QH):
                qs = slice(hf * QH, (hf + 1) * QH)
                cols = slice(mp * TQ + hf * QH, mp * TQ + (hf + 1) * QH)
                m_old = m_ref[mp, :, qs]
                m_new = jnp.maximum(m_old, mx_ref[:, cols])
                alpha = jnp.exp2(m_old - m_new)
                p = jnp.exp2(src_ref[:, cols] - m_new).astype(_BF16)
                pv = jnp.dot(vblk, p, preferred_element_type=_F32)
                acc_ref[mp, :, qs] = acc_ref[mp, :, qs] * alpha + pv[0:DK]
                l_ref[mp, :, qs] = l_ref[mp, :, qs] * alpha + pv[DK:DK + 1]
                m_ref[mp, :, qs] = m_new

    def next_diagonal():
        jn = jnp.minimum(qi + 1, n_q - 1)
        split_maps(qt_ref[0, :, pl.ds(pl.multiple_of(jn * TQ, TQ), TQ)], qzn_ref)
        scores(jn, sc_ref, mxc_ref, bias_idx=1, q_ref=qzn_ref)

    n_far = jnp.maximum(qi - 1, 0)
    odd_rest = jnp.bitwise_and(jnp.maximum(n_far - 1, 0), 1)

    @pl.when(qi <= 1)
    def _():
        scores(qi, sc_ref, mxc_ref, bias_idx=1)

    @pl.when(qi == 0)
    def _():
        consume(sc_ref, mxc_ref, qi)

    @pl.when(qi >= 1)
    def _():
        scores(qi - 1, sb_ref, mxb_ref, bias_idx=0)
        consume(sc_ref, mxc_ref, qi)

    @pl.when(n_far >= 1)
    def _():
        scores(0, sa_ref, mxa_ref)
        consume(sb_ref, mxb_ref, qi - 1)

    def far_pair(i, carry):
        f0 = 2 * i
        scores(f0 + 1, sb_ref, mxb_ref)
        consume(sa_ref, mxa_ref, f0)
        scores(f0 + 2, sa_ref, mxa_ref)
        consume(sb_ref, mxb_ref, f0 + 1)
        return carry

    lax.fori_loop(0, lax.shift_right_logical(jnp.maximum(n_far - 1, 0), 1), far_pair, 0)

    @pl.when(jnp.logical_and(n_far >= 1, odd_rest == 1))
    def _():
        scores(n_far - 1, sb_ref, mxb_ref)
        consume(sa_ref, mxa_ref, n_far - 2)

    last_in_a = jnp.logical_and(n_far >= 1, odd_rest == 0)

    @pl.when(last_in_a)
    def _():
        next_diagonal()
        consume(sa_ref, mxa_ref, n_far - 1)

    @pl.when(jnp.logical_and(qi >= 1, jnp.logical_not(last_in_a)))
    def _():
        next_diagonal()
        consume(sb_ref, mxb_ref, jnp.maximum(n_far, 1) - 1)

    lam = lam_ref[0, 0]
    o_t = acc_ref[0] * (1.0 / l_ref[0]) - lam * (acc_ref[1] * (1.0 / l_ref[1]))
    o = o_t.T
    o = _rms_norm_rows(o, sg_ref[...]) * (1.0 - lam_init)
    a_ref[0] = (o * _silu(ga_ref[0])).astype(_BF16)


def _prompt_attn(lam, qt, kb, vt, bias, ga, subln_g, lam_init):
    b, s, _ = kb.shape
    return pl.pallas_call(
        functools.partial(_prompt_attn_kernel, lam_init=lam_init),
        grid=(b, H_A, s // TQ),
        in_specs=[pl.BlockSpec(memory_space=pltpu.SMEM),
                  pl.BlockSpec((1, DK, s), lambda bi, h, qi: (bi, h, 0)),
                  pl.BlockSpec((1, s, DK), lambda bi, h, qi: (bi, 0, h)),
                  pl.BlockSpec((1, 1, VA, s), lambda bi, h, qi: (bi, h, 0, 0)),
                  pl.BlockSpec((1, 2, TK, TQ), lambda bi, h, qi: (h, 0, 0, 0)),
                  pl.BlockSpec((1, TQ, DK), lambda bi, h, qi: (bi, qi, h)),
                  pl.BlockSpec(subln_g.shape, lambda bi, h, qi: (0, 0))],
        out_specs=pl.BlockSpec((1, TQ, DK), lambda bi, h, qi: (bi, qi, h)),
        out_shape=jax.ShapeDtypeStruct((b, s, W_A), _BF16),
        scratch_shapes=[pltpu.VMEM((DK, 2 * TQ), _BF16),
                        pltpu.VMEM((DK, 2 * TQ), _BF16),
                        pltpu.VMEM((TK, 2 * TQ), _F32),
                        pltpu.VMEM((TK, 2 * TQ), _F32),
                        pltpu.VMEM((TK, 2 * TQ), _F32),
                        pltpu.VMEM((1, 2 * TQ), _F32),
                        pltpu.VMEM((1, 2 * TQ), _F32),
                        pltpu.VMEM((1, 2 * TQ), _F32),
                        pltpu.VMEM((2, 1, TQ), _F32),
                        pltpu.VMEM((2, 1, TQ), _F32),
                        pltpu.VMEM((2, DK, TQ), _F32)],
        compiler_params=pltpu.CompilerParams(dimension_semantics=("arbitrary", "arbitrary", "arbitrary"),
                                             vmem_limit_bytes=VMEM_LIMIT),
        name="prompt_attn",
    )(lam, qt, kb, vt, bias, ga, subln_g)


def _out_proj_kernel(a_ref, g_ref, x_ref, wa_ref, wg_ref, lng_ref, lnb_ref, y_ref, *, alpha):
    n_sub = 2 if a_ref.shape[0] % 16 == 0 else 1
    sub = a_ref.shape[0] // n_sub
    for i in range(n_sub):
        rows = slice(i * sub, (i + 1) * sub)
        out = (jnp.dot(a_ref[rows, :], wa_ref[...], preferred_element_type=_F32)
               + jnp.dot(g_ref[rows, :], wg_ref[...], preferred_element_type=_F32))
        z = alpha * x_ref[rows, :] + out
        y_ref[rows, :] = _layer_norm_rows(z, lng_ref[...], lnb_ref[...])


def _out_proj(a, g, x, wo_a, wo_g, ln_g, ln_b, alpha):
    n, d = x.shape
    tm = min(n, TM)
    row = lambda i: (i, 0)
    const = lambda i: (0, 0)
    return pl.pallas_call(
        functools.partial(_out_proj_kernel, alpha=alpha),
        grid=(n // tm,),
        in_specs=[pl.BlockSpec((tm, W_A), row), pl.BlockSpec((tm, W_B), row), pl.BlockSpec((tm, d), row),
                  pl.BlockSpec(wo_a.shape, const), pl.BlockSpec(wo_g.shape, const),
                  pl.BlockSpec(ln_g.shape, const), pl.BlockSpec(ln_b.shape, const)],
        out_specs=pl.BlockSpec((tm, d), row),
        out_shape=jax.ShapeDtypeStruct((n, d), _F32),
        compiler_params=pltpu.CompilerParams(dimension_semantics=("arbitrary",),
                                             vmem_limit_bytes=VMEM_LIMIT),
        name="out_proj",
    )(a, g, x, wo_a, wo_g, ln_g, ln_b)


def _sample_proj_kernel(x_ref, w_ref, lng_ref, lnb_ref, cs_ref, bs_ref, lp_ref,
                        q_ref, k_ref, v_ref, ga_ref, g_ref, vn_ref, lam_ref, *, lam_init, t_valid):
    rows = x_ref.shape[0]
    h = jnp.dot(x_ref[...].astype(_BF16), w_ref[...], preferred_element_type=_F32)
    q_ref[...] = h[:, 0:W_A]
    k_ref[...] = h[:, W_A:2 * W_A]
    v_ref[...] = h[:, 2 * W_A:3 * W_A]
    ga_ref[...] = h[:, 3 * W_A:4 * W_A]
    o = 4 * W_A
    u = h[:, o:o + W_B]
    vb = h[:, o + W_B:o + 2 * W_B]
    gb = h[:, o + 2 * W_B:o + 3 * W_B]
    vn = _layer_norm_rows(vb, lng_ref[...], lnb_ref[...])
    vn_ref[...] = vn
    vn3 = vn.reshape(rows // T_PAD, T_PAD, W_B)
    s3 = jnp.broadcast_to(bs_ref[...][None], vn3.shape)
    for j in range(t_valid):
        s3 = s3 + vn3[:, j:j + 1, :] * cs_ref[j][None]
    g_ref[...] = (u * s3.reshape(rows, W_B) * _silu(gb)).astype(_BF16)
    lp = lp_ref[...]
    e1 = jnp.exp(jnp.sum(lp[0:1] * lp[1:2], axis=-1, keepdims=True))
    e2 = jnp.exp(jnp.sum(lp[2:3] * lp[3:4], axis=-1, keepdims=True))
    lam_ref[...] = jnp.broadcast_to(e1 - e2 + lam_init, lam_ref.shape)


def _sample_proj(x, w, ln_g, ln_b, cs, bs8, lam_params, lam_init, t_valid):
    n, d = x.shape
    tm = min(n, 256)
    row = lambda i: (i, 0)
    const2 = lambda i: (0, 0)
    const3 = lambda i: (0, 0, 0)
    f32_rows = jax.ShapeDtypeStruct((n, W_A), _F32)
    return pl.pallas_call(
        functools.partial(_sample_proj_kernel, lam_init=lam_init, t_valid=t_valid),
        grid=(n // tm,),
        in_specs=[pl.BlockSpec((tm, d), row), pl.BlockSpec(w.shape, const2),
                  pl.BlockSpec(ln_g.shape, const2), pl.BlockSpec(ln_b.shape, const2),
                  pl.BlockSpec(cs.shape, const3), pl.BlockSpec(bs8.shape, const2),
                  pl.BlockSpec(lam_params.shape, const2)],
        out_specs=[pl.BlockSpec((tm, W_A), row)] * 4
                  + [pl.BlockSpec((tm, W_B), row), pl.BlockSpec((tm, W_B), row),
                     pl.BlockSpec((8, 128), const2)],
        out_shape=[f32_rows] * 4 + [jax.ShapeDtypeStruct((n, W_B), _BF16),
                                    jax.ShapeDtypeStruct((n, W_B), _F32),
                                    jax.ShapeDtypeStruct((8, 128), _F32)],
        compiler_params=pltpu.CompilerParams(dimension_semantics=("arbitrary",),
                                             vmem_limit_bytes=VMEM_LIMIT),
        name="sample_proj",
    )(x, w, ln_g, ln_b, cs, bs8, lam_params)


def _sample_attn_kernel(pt_ref, lam_ref, q_ref, kn_ref, vn_ref, ga_ref, hm_ref, neg_ref, bl_ref, bn_ref, sg_ref,
                        ck_ref, cv_ref, a_ref, kbuf, vbuf, ksem, vsem, qm_ref, s0_ref, s1_ref, m_ref, l_ref, acc_ref,
                        *, lam_init, n_groups, n_seq):
    gp = PAGES_PER_GROUP
    b = pl.program_id(0)
    n_rows = 2 * H_A * T_PAD
    page_rows = kbuf.shape[1]

    def group_copies(hbm_ref, buf, sem, seq, grp):
        slot = jnp.bitwise_and(grp, RING_GROUPS - 1)
        return [pltpu.make_async_copy(hbm_ref.at[pt_ref[seq, grp * gp + i]], buf.at[slot * gp + i], sem.at[slot])
                for i in range(gp)]

    def start_ahead(hbm_ref, buf, sem, grp):
        ahead = jnp.asarray(grp + RING_GROUPS, jnp.int32)
        wraps = (ahead >= n_groups).astype(jnp.int32)
        seq = b + wraps
        grp2 = ahead - wraps * n_groups

        @pl.when(seq < n_seq)
        def _():
            for c in group_copies(hbm_ref, buf, sem, seq, grp2):
                c.start()

    def wait_group(hbm_ref, buf, sem, grp):
        for c in group_copies(hbm_ref, buf, sem, b, grp):
            c.wait()

    @pl.when(b == 0)
    def _():
        for g in range(RING_GROUPS):
            for c in group_copies(ck_ref, kbuf, ksem, 0, g) + group_copies(cv_ref, vbuf, vsem, 0, g):
                c.start()

    q8 = q_ref[0]
    per_head = [q8[:, h * DK:(h + 1) * DK] for h in range(H_A)]
    qm_ref[...] = jnp.concatenate(per_head + per_head, axis=0) * hm_ref[...]
    m_ref[...] = jnp.full(m_ref.shape, NEG, _F32)
    l_ref[...] = jnp.zeros(l_ref.shape, _F32)
    acc_ref[...] = jnp.zeros(acc_ref.shape, _F32)

    def scores(grp, dst_ref, with_last_bias):
        slot = jnp.bitwise_and(grp, RING_GROUPS - 1)
        qm = qm_ref[...]
        for i in range(gp):
            s = lax.dot_general(qm, kbuf[slot * gp + i], _NT, preferred_element_type=_F32) + neg_ref[...]
            if with_last_bias and i == gp - 1:
                s = s + bl_ref[...]
            dst_ref[:, i * page_rows:(i + 1) * page_rows] = s

    def consume(grp, src_ref):
        slot = jnp.bitwise_and(grp, RING_GROUPS - 1)
        smax = src_ref[:, 0:page_rows]
        for i in range(1, gp):
            smax = jnp.maximum(smax, src_ref[:, i * page_rows:(i + 1) * page_rows])
        m_old = m_ref[...]
        m_new = jnp.maximum(m_old, jnp.max(smax, axis=-1, keepdims=True))
        alpha = jnp.exp2(m_old - m_new)
        l = alpha * l_ref[...]
        acc = alpha * acc_ref[...]
        for i in range(gp):
            p = jnp.exp2(src_ref[:, i * page_rows:(i + 1) * page_rows] - m_new)
            l = l + p
            acc = acc + jnp.dot(p, vbuf[slot * gp + i], preferred_element_type=_F32)
        l_ref[...] = l
        acc_ref[...] = acc
        m_ref[...] = m_new

    def block(grp, dst_ref, src_ref, with_last_bias=False):
        wait_group(ck_ref, kbuf, ksem, grp)
        wait_group(cv_ref, vbuf, vsem, grp - 1)
        scores(grp, dst_ref, with_last_bias)
        consume(grp - 1, src_ref)
        start_ahead(ck_ref, kbuf, ksem, grp)
        start_ahead(cv_ref, vbuf, vsem, grp - 1)

    wait_group(ck_ref, kbuf, ksem, 0)
    scores(0, s0_ref, False)
    start_ahead(ck_ref, kbuf, ksem, 0)

    def pair(i, carry):
        g = 2 * i + 1
        block(g, s1_ref, s0_ref)
        block(g + 1, s0_ref, s1_ref)
        return carry

    lax.fori_loop(0, (n_groups - 2) // 2, pair, 0)
    block(n_groups - 1, s1_ref, s0_ref, with_last_bias=True)
    wait_group(cv_ref, vbuf, vsem, n_groups - 1)
    consume(n_groups - 1, s1_ref)
    start_ahead(cv_ref, vbuf, vsem, n_groups - 1)

    sn = lax.dot_general(qm_ref[...], kn_ref[0], _NT, preferred_element_type=_F32) + bn_ref[...]
    m_o = m_ref[...]
    m_f = jnp.maximum(m_o, jnp.max(sn, axis=-1, keepdims=True))
    al = jnp.exp2(m_o - m_f)
    pn = jnp.exp2(sn - m_f)
    l_tot = al * jnp.sum(l_ref[...], axis=-1, keepdims=True) + jnp.sum(pn, axis=-1, keepdims=True)
    acc = al * acc_ref[...] + jnp.dot(pn, vn_ref[0], preferred_element_type=_F32)
    o_full = acc * (1.0 / l_tot)
    half = n_rows // 2
    od = o_full[0:half] - lam_ref[0, 0] * o_full[half:n_rows]
    ga = ga_ref[0]
    for h in range(H_A):
        cols = slice(h * DK, (h + 1) * DK)
        oh = _rms_norm_rows(od[h * T_PAD:(h + 1) * T_PAD], sg_ref[...]) * (1.0 - lam_init)
        a_ref[0, :, cols] = (oh * _silu(ga[:, cols])).astype(_BF16)


def _sample_attn(page_table, lam, q8, kn, vn, ga8, halfmask, negmask, bias_last, bias_new, subln_g, ck, cv,
                 lam_init):
    db, n_pages = page_table.shape
    gp = PAGES_PER_GROUP
    n_groups = n_pages // gp
    assert n_pages % gp == 0 and n_groups % RING_GROUPS == 0 and n_groups % 2 == 0
    page_rows = ck.shape[1]
    n_rows = 2 * H_A * T_PAD
    seq = lambda bi, pt: (bi, 0, 0)
    const = lambda bi, pt: (0, 0)
    grid_spec = pltpu.PrefetchScalarGridSpec(
        num_scalar_prefetch=1,
        grid=(db,),
        in_specs=[pl.BlockSpec(memory_space=pltpu.SMEM),
                  pl.BlockSpec((1, T_PAD, W_A), seq),
                  pl.BlockSpec((1, T_PAD * H_A, DK), seq), pl.BlockSpec((1, T_PAD * H_A, DK), seq),
                  pl.BlockSpec((1, T_PAD, W_A), seq),
                  pl.BlockSpec(halfmask.shape, const), pl.BlockSpec(negmask.shape, const),
                  pl.BlockSpec(bias_last.shape, const), pl.BlockSpec(bias_new.shape, const),
                  pl.BlockSpec(subln_g.shape, const),
                  pl.BlockSpec(memory_space=pl.ANY), pl.BlockSpec(memory_space=pl.ANY)],
        out_specs=pl.BlockSpec((1, T_PAD, W_A), seq),
        scratch_shapes=[pltpu.VMEM((RING_GROUPS * gp, page_rows, DK), _F32),
                        pltpu.VMEM((RING_GROUPS * gp, page_rows, DK), _F32),
                        pltpu.SemaphoreType.DMA((RING_GROUPS,)),
                        pltpu.SemaphoreType.DMA((RING_GROUPS,)),
                        pltpu.VMEM((n_rows, DK), _F32),
                        pltpu.VMEM((n_rows, gp * page_rows), _F32),
                        pltpu.VMEM((n_rows, gp * page_rows), _F32),
                        pltpu.VMEM((n_rows, 1), _F32),
                        pltpu.VMEM((n_rows, page_rows), _F32),
                        pltpu.VMEM((n_rows, DK), _F32)],
    )
    return pl.pallas_call(
        functools.partial(_sample_attn_kernel, lam_init=lam_init, n_groups=n_groups, n_seq=db),
        grid_spec=grid_spec,
        out_shape=jax.ShapeDtypeStruct((db, T_PAD, W_A), _BF16),
        compiler_params=pltpu.CompilerParams(dimension_semantics=("arbitrary",),
                                             vmem_limit_bytes=VMEM_LIMIT),
        name="sample_attn",
    )(page_table, lam, q8, kn, vn, ga8, halfmask, negmask, bias_last, bias_new, subln_g, ck, cv)


def kernel(x_prompt, x_sample, cache_k, cache_v, page_table, w_in, w_out, lambda_q1, lambda_k1, lambda_q2,
           lambda_k2, subln_g, rel_bias, sgu_ln_g, sgu_ln_b, sgu_w, sgu_b, post_ln_g, post_ln_b):
    depth = w_in.shape[0]
    assert depth == 1, "single trunk layer"
    b, s, d = x_prompt.shape
    db, t, _ = x_sample.shape
    n_pool, page = cache_k.shape[1], cache_k.shape[2]
    assert s % TQ == 0 and s % TM == 0 and TQ == TK and TQ >= MAX_DISTANCE
    assert t <= T_PAD and page >= MAX_DISTANCE
    lam_init = 0.8 - 0.6 * math.exp(-0.3 * 0)
    alpha = (2.0 * depth) ** 0.25

    col_scale = jnp.where(jnp.arange(w_in.shape[2]) < W_A, SCALE * LOG2E, 1.0).astype(_F32)
    w_all = (w_in[0] * col_scale[None, :]).astype(_BF16)
    wo_a = w_out[0, 0:W_A].astype(_BF16)
    wo_g = w_out[0, W_A:].astype(_BF16)
    ln_g = sgu_ln_g[0][None]
    ln_b = sgu_ln_b[0][None]
    pg_ = post_ln_g[0][None]
    pb_ = post_ln_b[0][None]
    sg = subln_g[0][None]
    tril = jnp.tril(jnp.ones((CHUNK, CHUNK), _F32))
    wt = (sgu_w[0] * tril).astype(_BF16)
    bsb = jnp.broadcast_to(sgu_b[0][:, :, None], (H_B, CHUNK, CB))
    wsmall = sgu_w[0][:, :T_PAD, :T_PAD] * tril[:T_PAD, :T_PAD]
    valid_t = (jnp.arange(T_PAD) < t).astype(_F32)
    cs = jnp.transpose(wsmall, (2, 1, 0)) * valid_t[None, :, None]
    cs = jnp.repeat(cs, CB, axis=2)
    bs8 = jnp.repeat(jnp.transpose(sgu_b[0][:, :T_PAD]) * valid_t[:, None], CB, axis=1)
    lam_params = jnp.concatenate([lambda_q1, lambda_k1, lambda_q2, lambda_k2], axis=0)

    kk = np.arange(TK)[:, None]
    qq = np.arange(TQ)[None, :]
    idx_sub = _bucket_np(TQ + qq - kk)
    idx_diag = np.where(qq >= kk, _bucket_np(qq - kk), MASKED_BUCKET)
    bias_prompt = _bias_tiles(rel_bias, jnp.asarray(np.stack([idx_sub, idx_diag]).astype(np.int32)))
    tt = np.arange(T_PAD)[:, None]
    ii = np.arange(page * H_A)[None, :] // H_A
    idx_last = _bucket_np(np.where(tt < t, page + tt - ii, MAX_DISTANCE))
    idx_new = np.where((ii <= tt) & (tt < t), _bucket_np(tt - ii), MASKED_BUCKET)
    bias_sample = _bias_tiles(rel_bias, jnp.asarray(np.stack([idx_last, idx_new]).astype(np.int32)))
    r = np.arange(2 * H_A * T_PAD)
    r_head = (r // T_PAD) % H_A
    r_map = r // (H_A * T_PAD)
    lane_head = np.arange(page * H_A) % H_A
    own_head = r_head[:, None] == lane_head[None, :]
    negmask = jnp.asarray(np.where(own_head, 0.0, NEG).astype(np.float32))
    halfmask = jnp.asarray((np.arange(DK)[None, :] // HD == r_map[:, None]).astype(np.float32))
    bias_last = jnp.tile(bias_sample[:, 0].reshape(H_A * T_PAD, page * H_A), (2, 1))
    bias_new = (jnp.tile(bias_sample[:, 1, :, :T_PAD * H_A].reshape(H_A * T_PAD, T_PAD * H_A), (2, 1))
                + negmask[:, :T_PAD * H_A])

    xs = jnp.pad(x_sample, ((0, 0), (0, T_PAD - t), (0, 0))).reshape(db * T_PAD, d)
    q8, k8, v8, ga8, g8, vn8, lam_tile = _sample_proj(xs, w_all, ln_g, ln_b, cs, bs8, lam_params, lam_init, t)
    lam = lam_tile[0:1, 0:1]
    ck = cache_k[0].reshape(n_pool, page * H_A, DK)
    cv = cache_v[0].reshape(n_pool, page * H_A, DK)
    a8 = _sample_attn(page_table, lam, q8.reshape(db, T_PAD, W_A), k8.reshape(db, T_PAD * H_A, DK),
                      v8.reshape(db, T_PAD * H_A, DK), ga8.reshape(db, T_PAD, W_A), halfmask, negmask,
                      bias_last, bias_new, sg, ck, cv, lam_init)
    ys = _out_proj(a8.reshape(db * T_PAD, W_A), g8, xs, wo_a, wo_g, pg_, pb_, alpha)
    y_sample = ys.reshape(db, T_PAD, d)[:, :t]
    nks = k8.reshape(db, T_PAD, H_A, DK)[None, :, :t]
    nvs = v8.reshape(db, T_PAD, H_A, DK)[None, :, :t]
    nsv = vn8.reshape(db, T_PAD, H_B, CB)[None, :, :t]

    kf, vf, kb, qt, vt, ga, g = _prompt_proj(x_prompt, w_all, ln_g, ln_b, wt, bsb)
    a = _prompt_attn(lam, qt, kb, vt, bias_prompt, ga, sg, lam_init)
    yp = _out_proj(a.reshape(b * s, W_A), g.reshape(b * s, W_B), x_prompt.reshape(b * s, d),
                   wo_a, wo_g, pg_, pb_, alpha)
    y_prompt = yp.reshape(b, s, d)
    nkp = kf.reshape(1, b, s, H_A, DK)
    nvp = vf.reshape(1, b, s, H_A, DK)
    return (y_prompt, y_sample, nkp, nvp, nks, nvs, nsv)
```

```python
import functools
import math

import numpy as np
import jax
import jax.numpy as jnp
from jax import lax
from jax.experimental import pallas as pl
from jax.experimental.pallas import tpu as pltpu

H_A = 4
HD = 64
DK = 2 * HD
W_A = H_A * DK
H_B = 4
CB = 128
W_B = H_B * CB
CHUNK = 128
N_BUCKETS = 32
MAX_DISTANCE = 128
LN_EPS = 1e-5
SCALE = HD ** -0.5
LOG2E = math.log2(math.e)
NEG = -1e30
MASKED_BUCKET = N_BUCKETS

TM = 512
TM_OUT = 1024
OUT_SUB = 256
TQ = 512
TK = 512
QH = TQ // 2
ONES_ROWS = 16
VA = DK + ONES_ROWS
T_PAD = 8
PAGES_PER_GROUP = 4
RING_GROUPS = 4
VMEM_LIMIT = 56 * 1024 * 1024

_F32 = jnp.float32
_BF16 = jnp.bfloat16
_NT = (((1,), (1,)), ((), ()))


def _bucket_np(dist):
    n = np.maximum(dist, 0)
    max_exact = N_BUCKETS // 2
    nf = np.maximum(n, 1).astype(np.float32)
    large = max_exact + (np.log(nf / max_exact) / math.log(MAX_DISTANCE / max_exact)
                         * (N_BUCKETS - max_exact)).astype(np.int32)
    large = np.minimum(large, N_BUCKETS - 1)
    return np.where(n < max_exact, n, large).astype(np.int32)


def _silu(x):
    return x * jax.nn.sigmoid(x)


def _layer_norm_rows(x, g, b):
    mu = jnp.mean(x, axis=-1, keepdims=True)
    xc = x - mu
    var = jnp.mean(xc * xc, axis=-1, keepdims=True)
    return xc * lax.rsqrt(var + LN_EPS) * g + b


def _rms_norm_rows(x, g):
    return x * lax.rsqrt(jnp.mean(x * x, axis=-1, keepdims=True) + LN_EPS) * g


def _bias_kernel(tab_ref, idx_ref, o_ref):
    h = pl.program_id(0)
    idx = idx_ref[0]
    far = tab_ref[N_BUCKETS - 1, h]
    acc = jnp.full(idx.shape, NEG, _F32)
    for b in range(N_BUCKETS):
        acc = jnp.where(idx == b, (tab_ref[b, h] - far) * LOG2E, acc)
    o_ref[0, 0] = acc


def _bias_tiles(table, idx):
    n, r, c = idx.shape
    return pl.pallas_call(
        _bias_kernel,
        grid=(H_A, n),
        in_specs=[pl.BlockSpec(memory_space=pltpu.SMEM),
                  pl.BlockSpec((1, r, c), lambda h, i: (i, 0, 0))],
        out_specs=pl.BlockSpec((1, 1, r, c), lambda h, i: (h, i, 0, 0)),
        out_shape=jax.ShapeDtypeStruct((H_A, n, r, c), _F32),
        name="bias_tiles",
    )(table, idx)


def _prompt_proj_kernel(x_ref, w_ref, lng_ref, lnb_ref, wt_ref, bs_ref,
                        kf_ref, vf_ref, kb_ref, qt_ref, vt_ref, ga_ref, g_ref):
    xb = x_ref[0].astype(_BF16)
    hm = jnp.dot(xb, w_ref[...], preferred_element_type=_F32)
    k = hm[:, W_A:2 * W_A]
    v = hm[:, 2 * W_A:3 * W_A]
    for h in range(H_A):
        cols = slice(h * DK, (h + 1) * DK)
        kf_ref[0, pl.ds(h, TM, stride=H_A), :] = k[:, cols]
        vf_ref[0, pl.ds(h, TM, stride=H_A), :] = v[:, cols]
    kb_ref[0] = k.astype(_BF16)
    ga_ref[0] = hm[:, 3 * W_A:4 * W_A]
    qt_ref[0] = hm[:, 0:W_A].T.astype(_BF16)
    vt = v.T.astype(_BF16)
    for h in range(H_A):
        vt_ref[0, h, 0:DK, :] = vt[h * DK:(h + 1) * DK]
        vt_ref[0, h, DK:VA, :] = jnp.ones((ONES_ROWS, TM), _BF16)
    o = 4 * W_A
    u = hm[:, o:o + W_B]
    vb = hm[:, o + W_B:o + 2 * W_B]
    gb = hm[:, o + 2 * W_B:o + 3 * W_B]
    vn = _layer_norm_rows(vb, lng_ref[...], lnb_ref[...]).astype(_BF16)
    gate = u * _silu(gb)
    n_chunks = TM // CHUNK
    for h in range(H_B):
        cols = slice(h * CB, (h + 1) * CB)
        vn_h = jnp.concatenate([vn[c * CHUNK:(c + 1) * CHUNK, cols] for c in range(n_chunks)], axis=1)
        s = jnp.dot(wt_ref[h], vn_h, preferred_element_type=_F32)
        for c in range(n_chunks):
            rows = slice(c * CHUNK, (c + 1) * CHUNK)
            g_ref[0, rows, cols] = (gate[rows, cols] * (s[:, c * CB:(c + 1) * CB] + bs_ref[h])).astype(_BF16)


def _prompt_proj(x, w_all, ln_g, ln_b, wt, bsb):
    b, s, d = x.shape
    row = lambda bi, si: (bi, si, 0)
    col = lambda bi, si: (bi, 0, si)
    const2 = lambda bi, si: (0, 0)
    const3 = lambda bi, si: (0, 0, 0)
    f32_heads = jax.ShapeDtypeStruct((b, s * H_A, DK), _F32)
    bf_rows = jax.ShapeDtypeStruct((b, s, W_A), _BF16)
    bf_cols = jax.ShapeDtypeStruct((b, W_A, s), _BF16)
    return pl.pallas_call(
        _prompt_proj_kernel,
        grid=(b, s // TM),
        in_specs=[pl.BlockSpec((1, TM, d), row),
                  pl.BlockSpec(w_all.shape, const2),
                  pl.BlockSpec(ln_g.shape, const2),
                  pl.BlockSpec(ln_b.shape, const2),
                  pl.BlockSpec(wt.shape, const3),
                  pl.BlockSpec(bsb.shape, const3)],
        out_specs=[pl.BlockSpec((1, TM * H_A, DK), row), pl.BlockSpec((1, TM * H_A, DK), row),
                   pl.BlockSpec((1, TM, W_A), row),
                   pl.BlockSpec((1, W_A, TM), col),
                   pl.BlockSpec((1, H_A, VA, TM), lambda bi, si: (bi, 0, 0, si)),
                   pl.BlockSpec((1, TM, W_A), row), pl.BlockSpec((1, TM, W_B), row)],
        out_shape=[f32_heads, f32_heads, bf_rows, bf_cols,
                   jax.ShapeDtypeStruct((b, H_A, VA, s), _BF16),
                   jax.ShapeDtypeStruct((b, s, W_A), _F32), jax.ShapeDtypeStruct((b, s, W_B), _BF16)],
        compiler_params=pltpu.CompilerParams(dimension_semantics=("arbitrary", "arbitrary"),
                                             vmem_limit_bytes=VMEM_LIMIT),
        name="prompt_proj",
    )(x, w_all, ln_g, ln_b, wt, bsb)


def _prompt_attn_kernel(lam_ref, qt_ref, k_ref, vt_ref, bias_ref, ga_ref, sg_ref, a_ref,
                        qz_ref, qzn_ref, sa_ref, sb_ref, sc_ref, mxa_ref, mxb_ref, mxc_ref, m_ref, l_ref, acc_ref,
                        *, lam_init):
    qi = pl.program_id(2)
    n_q = pl.num_programs(2)

    def split_maps(qt, dst_ref):
        row = lax.broadcasted_iota(jnp.int32, qt.shape, 0)
        zero = jnp.zeros_like(qt)
        dst_ref[:, 0:TQ] = jnp.where(row < HD, qt, zero)
        dst_ref[:, TQ:2 * TQ] = jnp.where(row >= HD, qt, zero)

    q_start = pl.multiple_of(qi * TQ, TQ)
    split_maps(qt_ref[0, :, pl.ds(q_start, TQ)], qz_ref)
    m_ref[...] = jnp.full(m_ref.shape, NEG, _F32)
    l_ref[...] = jnp.zeros(l_ref.shape, _F32)
    acc_ref[...] = jnp.zeros(acc_ref.shape, _F32)

    def scores(j, dst_ref, mx_ref, bias_idx=None, q_ref=qz_ref):
        start = pl.multiple_of(j * TK, TK)
        s = jnp.dot(k_ref[0, pl.ds(start, TK), :], q_ref[...], preferred_element_type=_F32)
        if bias_idx is not None:
            bias = bias_ref[0, bias_idx]
            s = jnp.concatenate([s[:, 0:TQ] + bias, s[:, TQ:2 * TQ] + bias], axis=1)
        dst_ref[...] = s
        mx_ref[...] = jnp.max(s, axis=0, keepdims=True)

    def consume(src_ref, mx_ref, j):
        start = pl.multiple_of(j * TK, TK)
        vblk = vt_ref[0, 0, :, pl.ds(start, TK)]
        for mp in range(2):
            for hf in range(TQ // QH):
                qs = slice(hf * QH, (hf + 1) * QH)
                cols = slice(mp * TQ + hf * QH, mp * TQ + (hf + 1) * QH)
                m_old = m_ref[mp, :, qs]
                m_new = jnp.maximum(m_old, mx_ref[:, cols])
                alpha = jnp.exp2(m_old - m_new)
                p = jnp.exp2(src_ref[:, cols] - m_new).astype(_BF16)
                pv = jnp.dot(vblk, p, preferred_element_type=_F32)
                acc_ref[mp, :, qs] = acc_ref[mp, :, qs] * alpha + pv[0:DK]
                l_ref[mp, :, qs] = l_ref[mp, :, qs] * alpha + pv[DK:DK + 1]
                m_ref[mp, :, qs] = m_new

    def next_diagonal():
        jn = jnp.minimum(qi + 1, n_q - 1)
        split_maps(qt_ref[0, :, pl.ds(pl.multiple_of(jn * TQ, TQ), TQ)], qzn_ref)
        scores(jn, sc_ref, mxc_ref, bias_idx=1, q_ref=qzn_ref)

    n_far = jnp.maximum(qi - 1, 0)
    odd_rest = jnp.bitwise_and(jnp.maximum(n_far - 1, 0), 1)

    @pl.when(qi <= 1)
    def _():
        scores(qi, sc_ref, mxc_ref, bias_idx=1)

    @pl.when(qi == 0)
    def _():
        consume(sc_ref, mxc_ref, qi)

    @pl.when(qi >= 1)
    def _():
        scores(qi - 1, sb_ref, mxb_ref, bias_idx=0)
        consume(sc_ref, mxc_ref, qi)

    @pl.when(n_far >= 1)
    def _():
        scores(0, sa_ref, mxa_ref)
        consume(sb_ref, mxb_ref, qi - 1)

    def far_pair(i, carry):
        f0 = 2 * i
        scores(f0 + 1, sb_ref, mxb_ref)
        consume(sa_ref, mxa_ref, f0)
        scores(f0 + 2, sa_ref, mxa_ref)
        consume(sb_ref, mxb_ref, f0 + 1)
        return carry

    lax.fori_loop(0, lax.shift_right_logical(jnp.maximum(n_far - 1, 0), 1), far_pair, 0)

    @pl.when(jnp.logical_and(n_far >= 1, odd_rest == 1))
    def _():
        scores(n_far - 1, sb_ref, mxb_ref)
        consume(sa_ref, mxa_ref, n_far - 2)

    last_in_a = jnp.logical_and(n_far >= 1, odd_rest == 0)

    @pl.when(last_in_a)
    def _():
        next_diagonal()
        consume(sa_ref, mxa_ref, n_far - 1)

    @pl.when(jnp.logical_and(qi >= 1, jnp.logical_not(last_in_a)))
    def _():
        next_diagonal()
        consume(sb_ref, mxb_ref, jnp.maximum(n_far, 1) - 1)

    lam = lam_ref[0, 0]
    o_t = acc_ref[0] * (1.0 / l_ref[0]) - lam * (acc_ref[1] * (1.0 / l_ref[1]))
    o = o_t.T
    o = _rms_norm_rows(o, sg_ref[...]) * (1.0 - lam_init)
    a_ref[0] = (o * _silu(ga_ref[0])).astype(_BF16)


def _prompt_attn(lam, qt, kb, vt, bias, ga, subln_g, lam_init):
    b, s, _ = kb.shape
    return pl.pallas_call(
        functools.partial(_prompt_attn_kernel, lam_init=lam_init),
        grid=(b, H_A, s // TQ),
        in_specs=[pl.BlockSpec(memory_space=pltpu.SMEM),
                  pl.BlockSpec((1, DK, s), lambda bi, h, qi: (bi, h, 0)),
                  pl.BlockSpec((1, s, DK), lambda bi, h, qi: (bi, 0, h)),
                  pl.BlockSpec((1, 1, VA, s), lambda bi, h, qi: (bi, h, 0, 0)),
                  pl.BlockSpec((1, 2, TK, TQ), lambda bi, h, qi: (h, 0, 0, 0)),
                  pl.BlockSpec((1, TQ, DK), lambda bi, h, qi: (bi, qi, h)),
                  pl.BlockSpec(subln_g.shape, lambda bi, h, qi: (0, 0))],
        out_specs=pl.BlockSpec((1, TQ, DK), lambda bi, h, qi: (bi, qi, h)),
        out_shape=jax.ShapeDtypeStruct((b, s, W_A), _BF16),
        scratch_shapes=[pltpu.VMEM((DK, 2 * TQ), _BF16),
                        pltpu.VMEM((DK, 2 * TQ), _BF16),
                        pltpu.VMEM((TK, 2 * TQ), _F32),
                        pltpu.VMEM((TK, 2 * TQ), _F32),
                        pltpu.VMEM((TK, 2 * TQ), _F32),
                        pltpu.VMEM((1, 2 * TQ), _F32),
                        pltpu.VMEM((1, 2 * TQ), _F32),
                        pltpu.VMEM((1, 2 * TQ), _F32),
                        pltpu.VMEM((2, 1, TQ), _F32),
                        pltpu.VMEM((2, 1, TQ), _F32),
                        pltpu.VMEM((2, DK, TQ), _F32)],
        compiler_params=pltpu.CompilerParams(dimension_semantics=("arbitrary", "arbitrary", "arbitrary"),
                                             vmem_limit_bytes=VMEM_LIMIT),
        name="prompt_attn",
    )(lam, qt, kb, vt, bias, ga, subln_g)


def _out_proj_kernel(a_ref, g_ref, x_ref, wa_ref, wg_ref, lng_ref, lnb_ref, y_ref, *, alpha):
    sub = min(OUT_SUB, a_ref.shape[0])
    for i in range(a_ref.shape[0] // sub):
        rows = slice(i * sub, (i + 1) * sub)
        out = (jnp.dot(a_ref[rows, :], wa_ref[...], preferred_element_type=_F32)
               + jnp.dot(g_ref[rows, :], wg_ref[...], preferred_element_type=_F32))
        z = alpha * x_ref[rows, :] + out
        y_ref[rows, :] = _layer_norm_rows(z, lng_ref[...], lnb_ref[...])


def _out_proj(a, g, x, wo_a, wo_g, ln_g, ln_b, alpha):
    n, d = x.shape
    tm = min(n, TM_OUT)
    row = lambda i: (i, 0)
    const = lambda i: (0, 0)
    return pl.pallas_call(
        functools.partial(_out_proj_kernel, alpha=alpha),
        grid=(n // tm,),
        in_specs=[pl.BlockSpec((tm, W_A), row), pl.BlockSpec((tm, W_B), row), pl.BlockSpec((tm, d), row),
                  pl.BlockSpec(wo_a.shape, const), pl.BlockSpec(wo_g.shape, const),
                  pl.BlockSpec(ln_g.shape, const), pl.BlockSpec(ln_b.shape, const)],
        out_specs=pl.BlockSpec((tm, d), row),
        out_shape=jax.ShapeDtypeStruct((n, d), _F32),
        compiler_params=pltpu.CompilerParams(dimension_semantics=("arbitrary",),
                                             vmem_limit_bytes=VMEM_LIMIT),
        name="out_proj",
    )(a, g, x, wo_a, wo_g, ln_g, ln_b)


def _sample_proj_kernel(x_ref, w_ref, lng_ref, lnb_ref, cs_ref, bs_ref, lp_ref,
                        q_ref, k_ref, v_ref, ga_ref, g_ref, vn_ref, lam_ref, *, lam_init, t_valid):
    rows = x_ref.shape[0]
    h = jnp.dot(x_ref[...].astype(_BF16), w_ref[...], preferred_element_type=_F32)
    q_ref[...] = h[:, 0:W_A]
    k_ref[...] = h[:, W_A:2 * W_A]
    v_ref[...] = h[:, 2 * W_A:3 * W_A]
    ga_ref[...] = h[:, 3 * W_A:4 * W_A]
    o = 4 * W_A
    u = h[:, o:o + W_B]
    vb = h[:, o + W_B:o + 2 * W_B]
    gb = h[:, o + 2 * W_B:o + 3 * W_B]
    vn = _layer_norm_rows(vb, lng_ref[...], lnb_ref[...])
    vn_ref[...] = vn
    vn3 = vn.reshape(rows // T_PAD, T_PAD, W_B)
    s3 = jnp.broadcast_to(bs_ref[...][None], vn3.shape)
    for j in range(t_valid):
        s3 = s3 + vn3[:, j:j + 1, :] * cs_ref[j][None]
    g_ref[...] = (u * s3.reshape(rows, W_B) * _silu(gb)).astype(_BF16)
    lp = lp_ref[...]
    e1 = jnp.exp(jnp.sum(lp[0:1] * lp[1:2], axis=-1, keepdims=True))
    e2 = jnp.exp(jnp.sum(lp[2:3] * lp[3:4], axis=-1, keepdims=True))
    lam_ref[...] = jnp.broadcast_to(e1 - e2 + lam_init, lam_ref.shape)


def _sample_proj(x, w, ln_g, ln_b, cs, bs8, lam_params, lam_init, t_valid):
    n, d = x.shape
    tm = min(n, 256)
    row = lambda i: (i, 0)
    const2 = lambda i: (0, 0)
    const3 = lambda i: (0, 0, 0)
    f32_rows = jax.ShapeDtypeStruct((n, W_A), _F32)
    return pl.pallas_call(
        functools.partial(_sample_proj_kernel, lam_init=lam_init, t_valid=t_valid),
        grid=(n // tm,),
        in_specs=[pl.BlockSpec((tm, d), row), pl.BlockSpec(w.shape, const2),
                  pl.BlockSpec(ln_g.shape, const2), pl.BlockSpec(ln_b.shape, const2),
                  pl.BlockSpec(cs.shape, const3), pl.BlockSpec(bs8.shape, const2),
                  pl.BlockSpec(lam_params.shape, const2)],
        out_specs=[pl.BlockSpec((tm, W_A), row)] * 4
                  + [pl.BlockSpec((tm, W_B), row), pl.BlockSpec((tm, W_B), row),
                     pl.BlockSpec((8, 128), const2)],
        out_shape=[f32_rows] * 4 + [jax.ShapeDtypeStruct((n, W_B), _BF16),
                                    jax.ShapeDtypeStruct((n, W_B), _F32),
                                    jax.ShapeDtypeStruct((8, 128), _F32)],
        compiler_params=pltpu.CompilerParams(dimension_semantics=("arbitrary",),
                                             vmem_limit_bytes=VMEM_LIMIT),
        name="sample_proj",
    )(x, w, ln_g, ln_b, cs, bs8, lam_params)


def _sample_attn_kernel(pt_ref, lam_ref, q_ref, kn_ref, vn_ref, ga_ref, hm_ref, neg_ref, bl_ref, bn_ref, sg_ref,
                        ck_ref, cv_ref, a_ref, kbuf, vbuf, ksem, vsem, qm_ref, s0_ref, s1_ref, m_ref, l_ref, acc_ref,
                        *, lam_init, n_groups, n_seq):
    gp = PAGES_PER_GROUP
    b = pl.program_id(0)
    n_rows = 2 * H_A * T_PAD
    page_rows = kbuf.shape[1]

    def group_copies(hbm_ref, buf, sem, seq, grp):
        slot = jnp.bitwise_and(grp, RING_GROUPS - 1)
        return [pltpu.make_async_copy(hbm_ref.at[pt_ref[seq, grp * gp + i]], buf.at[slot * gp + i], sem.at[slot])
                for i in range(gp)]

    def start_ahead(hbm_ref, buf, sem, grp):
        ahead = jnp.asarray(grp + RING_GROUPS, jnp.int32)
        wraps = (ahead >= n_groups).astype(jnp.int32)
        seq = b + wraps
        grp2 = ahead - wraps * n_groups

        @pl.when(seq < n_seq)
        def _():
            for i, c in enumerate(group_copies(hbm_ref, buf, sem, seq, grp2)):
                c.start(priority=i % 2)

    def wait_group(hbm_ref, buf, sem, grp):
        for c in group_copies(hbm_ref, buf, sem, b, grp):
            c.wait()

    @pl.when(b == 0)
    def _():
        for g in range(RING_GROUPS):
            for i, c in enumerate(group_copies(ck_ref, kbuf, ksem, 0, g) + group_copies(cv_ref, vbuf, vsem, 0, g)):
                c.start(priority=i % 2)

    q8 = q_ref[0]
    per_head = [q8[:, h * DK:(h + 1) * DK] for h in range(H_A)]
    qm_ref[...] = jnp.concatenate(per_head + per_head, axis=0) * hm_ref[...]
    m_ref[...] = jnp.full(m_ref.shape, NEG, _F32)
    l_ref[...] = jnp.zeros(l_ref.shape, _F32)
    acc_ref[...] = jnp.zeros(acc_ref.shape, _F32)

    def scores(grp, dst_ref, with_last_bias):
        slot = jnp.bitwise_and(grp, RING_GROUPS - 1)
        qm = qm_ref[...]
        for i in range(gp):
            s = lax.dot_general(qm, kbuf[slot * gp + i], _NT, preferred_element_type=_F32) + neg_ref[...]
            if with_last_bias and i == gp - 1:
                s = s + bl_ref[...]
            dst_ref[:, i * page_rows:(i + 1) * page_rows] = s

    def consume(grp, src_ref):
        slot = jnp.bitwise_and(grp, RING_GROUPS - 1)
        smax = src_ref[:, 0:page_rows]
        for i in range(1, gp):
            smax = jnp.maximum(smax, src_ref[:, i * page_rows:(i + 1) * page_rows])
        m_old = m_ref[...]
        m_new = jnp.maximum(m_old, jnp.max(smax, axis=-1, keepdims=True))
        alpha = jnp.exp2(m_old - m_new)
        l = alpha * l_ref[...]
        acc = alpha * acc_ref[...]
        for i in range(gp):
            p = jnp.exp2(src_ref[:, i * page_rows:(i + 1) * page_rows] - m_new)
            l = l + p
            acc = acc + jnp.dot(p, vbuf[slot * gp + i], preferred_element_type=_F32)
        l_ref[...] = l
        acc_ref[...] = acc
        m_ref[...] = m_new

    def block(grp, dst_ref, src_ref, with_last_bias=False):
        wait_group(ck_ref, kbuf, ksem, grp)
        wait_group(cv_ref, vbuf, vsem, grp - 1)
        scores(grp, dst_ref, with_last_bias)
        consume(grp - 1, src_ref)
        start_ahead(ck_ref, kbuf, ksem, grp)
        start_ahead(cv_ref, vbuf, vsem, grp - 1)

    wait_group(ck_ref, kbuf, ksem, 0)
    scores(0, s0_ref, False)
    start_ahead(ck_ref, kbuf, ksem, 0)

    def pair(i, carry):
        g = 2 * i + 1
        block(g, s1_ref, s0_ref)
        block(g + 1, s0_ref, s1_ref)
        return carry

    lax.fori_loop(0, (n_groups - 2) // 2, pair, 0)
    block(n_groups - 1, s1_ref, s0_ref, with_last_bias=True)
    wait_group(cv_ref, vbuf, vsem, n_groups - 1)
    consume(n_groups - 1, s1_ref)
    start_ahead(cv_ref, vbuf, vsem, n_groups - 1)

    sn = lax.dot_general(qm_ref[...], kn_ref[0], _NT, preferred_element_type=_F32) + bn_ref[...]
    m_o = m_ref[...]
    m_f = jnp.maximum(m_o, jnp.max(sn, axis=-1, keepdims=True))
    al = jnp.exp2(m_o - m_f)
    pn = jnp.exp2(sn - m_f)
    l_tot = al * jnp.sum(l_ref[...], axis=-1, keepdims=True) + jnp.sum(pn, axis=-1, keepdims=True)
    acc = al * acc_ref[...] + jnp.dot(pn, vn_ref[0], preferred_element_type=_F32)
    o_full = acc * (1.0 / l_tot)
    half = n_rows // 2
    od = o_full[0:half] - lam_ref[0, 0] * o_full[half:n_rows]
    ga = ga_ref[0]
    for h in range(H_A):
        cols = slice(h * DK, (h + 1) * DK)
        oh = _rms_norm_rows(od[h * T_PAD:(h + 1) * T_PAD], sg_ref[...]) * (1.0 - lam_init)
        a_ref[0, :, cols] = (oh * _silu(ga[:, cols])).astype(_BF16)


def _sample_attn(page_table, lam, q8, kn, vn, ga8, halfmask, negmask, bias_last, bias_new, subln_g, ck, cv,
                 lam_init):
    db, n_pages = page_table.shape
    gp = PAGES_PER_GROUP
    n_groups = n_pages // gp
    assert n_pages % gp == 0 and n_groups % RING_GROUPS == 0 and n_groups % 2 == 0
    page_rows = ck.shape[1]
    n_rows = 2 * H_A * T_PAD
    seq = lambda bi, pt: (bi, 0, 0)
    const = lambda bi, pt: (0, 0)
    grid_spec = pltpu.PrefetchScalarGridSpec(
        num_scalar_prefetch=1,
        grid=(db,),
        in_specs=[pl.BlockSpec(memory_space=pltpu.SMEM),
                  pl.BlockSpec((1, T_PAD, W_A), seq),
                  pl.BlockSpec((1, T_PAD * H_A, DK), seq), pl.BlockSpec((1, T_PAD * H_A, DK), seq),
                  pl.BlockSpec((1, T_PAD, W_A), seq),
                  pl.BlockSpec(halfmask.shape, const), pl.BlockSpec(negmask.shape, const),
                  pl.BlockSpec(bias_last.shape, const), pl.BlockSpec(bias_new.shape, const),
                  pl.BlockSpec(subln_g.shape, const),
                  pl.BlockSpec(memory_space=pl.ANY), pl.BlockSpec(memory_space=pl.ANY)],
        out_specs=pl.BlockSpec((1, T_PAD, W_A), seq),
        scratch_shapes=[pltpu.VMEM((RING_GROUPS * gp, page_rows, DK), _F32),
                        pltpu.VMEM((RING_GROUPS * gp, page_rows, DK), _F32),
                        pltpu.SemaphoreType.DMA((RING_GROUPS,)),
                        pltpu.SemaphoreType.DMA((RING_GROUPS,)),
                        pltpu.VMEM((n_rows, DK), _F32),
                        pltpu.VMEM((n_rows, gp * page_rows), _F32),
                        pltpu.VMEM((n_rows, gp * page_rows), _F32),
                        pltpu.VMEM((n_rows, 1), _F32),
                        pltpu.VMEM((n_rows, page_rows), _F32),
                        pltpu.VMEM((n_rows, DK), _F32)],
    )
    return pl.pallas_call(
        functools.partial(_sample_attn_kernel, lam_init=lam_init, n_groups=n_groups, n_seq=db),
        grid_spec=grid_spec,
        out_shape=jax.ShapeDtypeStruct((db, T_PAD, W_A), _BF16),
        compiler_params=pltpu.CompilerParams(dimension_semantics=("arbitrary",),
                                             vmem_limit_bytes=VMEM_LIMIT),
        name="sample_attn",
    )(page_table, lam, q8, kn, vn, ga8, halfmask, negmask, bias_last, bias_new, subln_g, ck, cv)


def kernel(x_prompt, x_sample, cache_k, cache_v, page_table, w_in, w_out, lambda_q1, lambda_k1, lambda_q2,
           lambda_k2, subln_g, rel_bias, sgu_ln_g, sgu_ln_b, sgu_w, sgu_b, post_ln_g, post_ln_b):
    depth = w_in.shape[0]
    assert depth == 1, "single trunk layer"
    b, s, d = x_prompt.shape
    db, t, _ = x_sample.shape
    n_pool, page = cache_k.shape[1], cache_k.shape[2]
    assert s % TQ == 0 and s % TM == 0 and TQ == TK and TQ >= MAX_DISTANCE
    assert t <= T_PAD and page >= MAX_DISTANCE
    lam_init = 0.8 - 0.6 * math.exp(-0.3 * 0)
    alpha = (2.0 * depth) ** 0.25

    col_scale = jnp.where(jnp.arange(w_in.shape[2]) < W_A, SCALE * LOG2E, 1.0).astype(_F32)
    w_all = (w_in[0] * col_scale[None, :]).astype(_BF16)
    wo_a = w_out[0, 0:W_A].astype(_BF16)
    wo_g = w_out[0, W_A:].astype(_BF16)
    ln_g = sgu_ln_g[0][None]
    ln_b = sgu_ln_b[0][None]
    pg_ = post_ln_g[0][None]
    pb_ = post_ln_b[0][None]
    sg = subln_g[0][None]
    tril = jnp.tril(jnp.ones((CHUNK, CHUNK), _F32))
    wt = (sgu_w[0] * tril).astype(_BF16)
    bsb = jnp.broadcast_to(sgu_b[0][:, :, None], (H_B, CHUNK, CB))
    wsmall = sgu_w[0][:, :T_PAD, :T_PAD] * tril[:T_PAD, :T_PAD]
    valid_t = (jnp.arange(T_PAD) < t).astype(_F32)
    cs = jnp.transpose(wsmall, (2, 1, 0)) * valid_t[None, :, None]
    cs = jnp.repeat(cs, CB, axis=2)
    bs8 = jnp.repeat(jnp.transpose(sgu_b[0][:, :T_PAD]) * valid_t[:, None], CB, axis=1)
    lam_params = jnp.concatenate([lambda_q1, lambda_k1, lambda_q2, lambda_k2], axis=0)

    kk = np.arange(TK)[:, None]
    qq = np.arange(TQ)[None, :]
    idx_sub = _bucket_np(TQ + qq - kk)
    idx_diag = np.where(qq >= kk, _bucket_np(qq - kk), MASKED_BUCKET)
    bias_prompt = _bias_tiles(rel_bias, jnp.asarray(np.stack([idx_sub, idx_diag]).astype(np.int32)))
    tt = np.arange(T_PAD)[:, None]
    ii = np.arange(page * H_A)[None, :] // H_A
    idx_last = _bucket_np(np.where(tt < t, page + tt - ii, MAX_DISTANCE))
    idx_new = np.where((ii <= tt) & (tt < t), _bucket_np(tt - ii), MASKED_BUCKET)
    bias_sample = _bias_tiles(rel_bias, jnp.asarray(np.stack([idx_last, idx_new]).astype(np.int32)))
    r = np.arange(2 * H_A * T_PAD)
    r_head = (r // T_PAD) % H_A
    r_map = r // (H_A * T_PAD)
    lane_head = np.arange(page * H_A) % H_A
    own_head = r_head[:, None] == lane_head[None, :]
    negmask = jnp.asarray(np.where(own_head, 0.0, NEG).astype(np.float32))
    halfmask = jnp.asarray((np.arange(DK)[None, :] // HD == r_map[:, None]).astype(np.float32))
    bias_last = jnp.tile(bias_sample[:, 0].reshape(H_A * T_PAD, page * H_A), (2, 1))
    bias_new = (jnp.tile(bias_sample[:, 1, :, :T_PAD * H_A].reshape(H_A * T_PAD, T_PAD * H_A), (2, 1))
                + negmask[:, :T_PAD * H_A])

    xs = jnp.pad(x_sample, ((0, 0), (0, T_PAD - t), (0, 0))).reshape(db * T_PAD, d)
    q8, k8, v8, ga8, g8, vn8, lam_tile = _sample_proj(xs, w_all, ln_g, ln_b, cs, bs8, lam_params, lam_init, t)
    lam = lam_tile[0:1, 0:1]
    ck = cache_k[0].reshape(n_pool, page * H_A, DK)
    cv = cache_v[0].reshape(n_pool, page * H_A, DK)
    a8 = _sample_attn(page_table, lam, q8.reshape(db, T_PAD, W_A), k8.reshape(db, T_PAD * H_A, DK),
                      v8.reshape(db, T_PAD * H_A, DK), ga8.reshape(db, T_PAD, W_A), halfmask, negmask,
                      bias_last, bias_new, sg, ck, cv, lam_init)
    ys = _out_proj(a8.reshape(db * T_PAD, W_A), g8, xs, wo_a, wo_g, pg_, pb_, alpha)
    y_sample = ys.reshape(db, T_PAD, d)[:, :t]
    nks = k8.reshape(db, T_PAD, H_A, DK)[None, :, :t]
    nvs = v8.reshape(db, T_PAD, H_A, DK)[None, :, :t]
    nsv = vn8.reshape(db, T_PAD, H_B, CB)[None, :, :t]

    kf, vf, kb, qt, vt, ga, g = _prompt_proj(x_prompt, w_all, ln_g, ln_b, wt, bsb)
    a = _prompt_attn(lam, qt, kb, vt, bias_prompt, ga, sg, lam_init)
    yp = _out_proj(a.reshape(b * s, W_A), g.reshape(b * s, W_B), x_prompt.reshape(b * s, d),
                   wo_a, wo_g, pg_, pb_, alpha)
    y_prompt = yp.reshape(b, s, d)
    nkp = kf.reshape(1, b, s, H_A, DK)
    nvp = vf.reshape(1, b, s, H_A, DK)
    return (y_prompt, y_sample, nkp, nvp, nks, nvs, nsv)
```

```python
import functools
import math

import numpy as np
import jax
import jax.numpy as jnp
from jax import lax
from jax.experimental import pallas as pl
from jax.experimental.pallas import tpu as pltpu

H_A = 4
HD = 64
DK = 2 * HD
W_A = H_A * DK
H_B = 4
CB = 128
W_B = H_B * CB
CHUNK = 128
N_BUCKETS = 32
MAX_DISTANCE = 128
LN_EPS = 1e-5
SCALE = HD ** -0.5
LOG2E = math.log2(math.e)
NEG = -1e30
MASKED_BUCKET = N_BUCKETS

TM = 512
TM_OUT = 1024
OUT_SUB = 256
TQ = 512
TK = 512
QH = TQ // 2
ONES_ROWS = 16
VA = DK + ONES_ROWS
T_PAD = 8
PAGES_PER_GROUP = 4
RING_GROUPS = 8
VMEM_LIMIT = 56 * 1024 * 1024

_F32 = jnp.float32
_BF16 = jnp.bfloat16
_NT = (((1,), (1,)), ((), ()))


def _bucket_np(dist):
    n = np.maximum(dist, 0)
    max_exact = N_BUCKETS // 2
    nf = np.maximum(n, 1).astype(np.float32)
    large = max_exact + (np.log(nf / max_exact) / math.log(MAX_DISTANCE / max_exact)
                         * (N_BUCKETS - max_exact)).astype(np.int32)
    large = np.minimum(large, N_BUCKETS - 1)
    return np.where(n < max_exact, n, large).astype(np.int32)


def _silu(x):
    return x * jax.nn.sigmoid(x)


def _layer_norm_rows(x, g, b):
    mu = jnp.mean(x, axis=-1, keepdims=True)
    xc = x - mu
    var = jnp.mean(xc * xc, axis=-1, keepdims=True)
    return xc * lax.rsqrt(var + LN_EPS) * g + b


def _rms_norm_rows(x, g):
    return x * lax.rsqrt(jnp.mean(x * x, axis=-1, keepdims=True) + LN_EPS) * g


def _bias_kernel(tab_ref, idx_ref, o_ref):
    h = pl.program_id(0)
    idx = idx_ref[0]
    far = tab_ref[N_BUCKETS - 1, h]
    acc = jnp.full(idx.shape, NEG, _F32)
    for b in range(N_BUCKETS):
        acc = jnp.where(idx == b, (tab_ref[b, h] - far) * LOG2E, acc)
    o_ref[0, 0] = acc


def _bias_tiles(table, idx):
    n, r, c = idx.shape
    return pl.pallas_call(
        _bias_kernel,
        grid=(H_A, n),
        in_specs=[pl.BlockSpec(memory_space=pltpu.SMEM),
                  pl.BlockSpec((1, r, c), lambda h, i: (i, 0, 0))],
        out_specs=pl.BlockSpec((1, 1, r, c), lambda h, i: (h, i, 0, 0)),
        out_shape=jax.ShapeDtypeStruct((H_A, n, r, c), _F32),
        name="bias_tiles",
    )(table, idx)


def _prompt_proj_kernel(x_ref, w_ref, lng_ref, lnb_ref, wt_ref, bs_ref,
                        kf_ref, vf_ref, kb_ref, qt_ref, vt_ref, ga_ref, g_ref):
    xb = x_ref[0].astype(_BF16)
    hm = jnp.dot(xb, w_ref[...], preferred_element_type=_F32)
    k = hm[:, W_A:2 * W_A]
    v = hm[:, 2 * W_A:3 * W_A]
    for h in range(H_A):
        cols = slice(h * DK, (h + 1) * DK)
        kf_ref[0, pl.ds(h, TM, stride=H_A), :] = k[:, cols]
        vf_ref[0, pl.ds(h, TM, stride=H_A), :] = v[:, cols]
    kb_ref[0] = k.astype(_BF16)
    ga_ref[0] = hm[:, 3 * W_A:4 * W_A]
    qt_ref[0] = hm[:, 0:W_A].T.astype(_BF16)
    vt = v.T.astype(_BF16)
    for h in range(H_A):
        vt_ref[0, h, 0:DK, :] = vt[h * DK:(h + 1) * DK]
        vt_ref[0, h, DK:VA, :] = jnp.ones((ONES_ROWS, TM), _BF16)
    o = 4 * W_A
    u = hm[:, o:o + W_B]
    vb = hm[:, o + W_B:o + 2 * W_B]
    gb = hm[:, o + 2 * W_B:o + 3 * W_B]
    vn = _layer_norm_rows(vb, lng_ref[...], lnb_ref[...]).astype(_BF16)
    gate = u * _silu(gb)
    n_chunks = TM // CHUNK
    for h in range(H_B):
        cols = slice(h * CB, (h + 1) * CB)
        vn_h = jnp.concatenate([vn[c * CHUNK:(c + 1) * CHUNK, cols] for c in range(n_chunks)], axis=1)
        s = jnp.dot(wt_ref[h], vn_h, preferred_element_type=_F32)
        for c in range(n_chunks):
            rows = slice(c * CHUNK, (c + 1) * CHUNK)
            g_ref[0, rows, cols] = (gate[rows, cols] * (s[:, c * CB:(c + 1) * CB] + bs_ref[h])).astype(_BF16)


def _prompt_proj(x, w_all, ln_g, ln_b, wt, bsb):
    b, s, d = x.shape
    row = lambda bi, si: (bi, si, 0)
    col = lambda bi, si: (bi, 0, si)
    const2 = lambda bi, si: (0, 0)
    const3 = lambda bi, si: (0, 0, 0)
    f32_heads = jax.ShapeDtypeStruct((b, s * H_A, DK), _F32)
    bf_rows = jax.ShapeDtypeStruct((b, s, W_A), _BF16)
    bf_cols = jax.ShapeDtypeStruct((b, W_A, s), _BF16)
    return pl.pallas_call(
        _prompt_proj_kernel,
        grid=(b, s // TM),
        in_specs=[pl.BlockSpec((1, TM, d), row),
                  pl.BlockSpec(w_all.shape, const2),
                  pl.BlockSpec(ln_g.shape, const2),
                  pl.BlockSpec(ln_b.shape, const2),
                  pl.BlockSpec(wt.shape, const3),
                  pl.BlockSpec(bsb.shape, const3)],
        out_specs=[pl.BlockSpec((1, TM * H_A, DK), row), pl.BlockSpec((1, TM * H_A, DK), row),
                   pl.BlockSpec((1, TM, W_A), row),
                   pl.BlockSpec((1, W_A, TM), col),
                   pl.BlockSpec((1, H_A, VA, TM), lambda bi, si: (bi, 0, 0, si)),
                   pl.BlockSpec((1, TM, W_A), row), pl.BlockSpec((1, TM, W_B), row)],
        out_shape=[f32_heads, f32_heads, bf_rows, bf_cols,
                   jax.ShapeDtypeStruct((b, H_A, VA, s), _BF16),
                   jax.ShapeDtypeStruct((b, s, W_A), _F32), jax.ShapeDtypeStruct((b, s, W_B), _BF16)],
        compiler_params=pltpu.CompilerParams(dimension_semantics=("arbitrary", "arbitrary"),
                                             vmem_limit_bytes=VMEM_LIMIT),
        name="prompt_proj",
    )(x, w_all, ln_g, ln_b, wt, bsb)


def _prompt_attn_kernel(lam_ref, qt_ref, k_ref, vt_ref, bias_ref, ga_ref, sg_ref, a_ref,
                        qz_ref, qzn_ref, sa_ref, sb_ref, sc_ref, mxa_ref, mxb_ref, mxc_ref, m_ref, l_ref, acc_ref,
                        *, lam_init):
    qi = pl.program_id(2)
    n_q = pl.num_programs(2)

    def split_maps(qt, dst_ref):
        row = lax.broadcasted_iota(jnp.int32, qt.shape, 0)
        zero = jnp.zeros_like(qt)
        dst_ref[:, 0:TQ] = jnp.where(row < HD, qt, zero)
        dst_ref[:, TQ:2 * TQ] = jnp.where(row >= HD, qt, zero)

    q_start = pl.multiple_of(qi * TQ, TQ)
    split_maps(qt_ref[0, :, pl.ds(q_start, TQ)], qz_ref)
    m_ref[...] = jnp.full(m_ref.shape, NEG, _F32)
    l_ref[...] = jnp.zeros(l_ref.shape, _F32)
    acc_ref[...] = jnp.zeros(acc_ref.shape, _F32)

    def scores(j, dst_ref, mx_ref, bias_idx=None, q_ref=qz_ref):
        start = pl.multiple_of(j * TK, TK)
        s = jnp.dot(k_ref[0, pl.ds(start, TK), :], q_ref[...], preferred_element_type=_F32)
        if bias_idx == 1:
            bias = bias_ref[0, 1]
            s = jnp.concatenate([s[:, 0:TQ] + bias, s[:, TQ:2 * TQ] + bias], axis=1)
        elif bias_idx == 0:
            md = MAX_DISTANCE
            corner = bias_ref[0, 0, TK - md:TK, 0:md]
            low = s[TK - md:TK]
            low = jnp.concatenate([low[:, 0:md] + corner, low[:, md:TQ],
                                   low[:, TQ:TQ + md] + corner, low[:, TQ + md:2 * TQ]], axis=1)
            s = jnp.concatenate([s[0:TK - md], low], axis=0)
        dst_ref[...] = s
        mx_ref[...] = jnp.max(s, axis=0, keepdims=True)

    def consume(src_ref, mx_ref, j):
        start = pl.multiple_of(j * TK, TK)
        vblk = vt_ref[0, 0, :, pl.ds(start, TK)]
        for mp in range(2):
            for hf in range(TQ // QH):
                qs = slice(hf * QH, (hf + 1) * QH)
                cols = slice(mp * TQ + hf * QH, mp * TQ + (hf + 1) * QH)
                m_old = m_ref[mp, :, qs]
                m_new = jnp.maximum(m_old, mx_ref[:, cols])
                alpha = jnp.exp2(m_old - m_new)
                p = jnp.exp2(src_ref[:, cols] - m_new).astype(_BF16)
                pv = jnp.dot(vblk, p, preferred_element_type=_F32)
                acc_ref[mp, :, qs] = acc_ref[mp, :, qs] * alpha + pv[0:DK]
                l_ref[mp, :, qs] = l_ref[mp, :, qs] * alpha + pv[DK:DK + 1]
                m_ref[mp, :, qs] = m_new

    def next_diagonal():
        jn = jnp.minimum(qi + 1, n_q - 1)
        split_maps(qt_ref[0, :, pl.ds(pl.multiple_of(jn * TQ, TQ), TQ)], qzn_ref)
        scores(jn, sc_ref, mxc_ref, bias_idx=1, q_ref=qzn_ref)

    n_far = jnp.maximum(qi - 1, 0)
    odd_rest = jnp.bitwise_and(jnp.maximum(n_far - 1, 0), 1)

    @pl.when(qi <= 1)
    def _():
        scores(qi, sc_ref, mxc_ref, bias_idx=1)

    @pl.when(qi == 0)
    def _():
        consume(sc_ref, mxc_ref, qi)

    @pl.when(qi >= 1)
    def _():
        scores(qi - 1, sb_ref, mxb_ref, bias_idx=0)
        consume(sc_ref, mxc_ref, qi)

    @pl.when(n_far >= 1)
    def _():
        scores(0, sa_ref, mxa_ref)
        consume(sb_ref, mxb_ref, qi - 1)

    def far_tiles(f0, count):
        for t in range(0, count, 2):
            scores(f0 + t + 1, sb_ref, mxb_ref)
            consume(sa_ref, mxa_ref, f0 + t)
            scores(f0 + t + 2, sa_ref, mxa_ref)
            consume(sb_ref, mxb_ref, f0 + t + 1)

    def far_quad(i, carry):
        far_tiles(4 * i, 4)
        return carry

    n_rest = jnp.maximum(n_far - 1, 0)
    n_quads = lax.shift_right_logical(n_rest, 2)
    lax.fori_loop(0, n_quads, far_quad, 0)

    @pl.when(jnp.bitwise_and(n_rest, 2) == 2)
    def _():
        far_tiles(4 * n_quads, 2)

    @pl.when(jnp.logical_and(n_far >= 1, odd_rest == 1))
    def _():
        scores(n_far - 1, sb_ref, mxb_ref)
        consume(sa_ref, mxa_ref, n_far - 2)

    last_in_a = jnp.logical_and(n_far >= 1, odd_rest == 0)

    @pl.when(last_in_a)
    def _():
        next_diagonal()
        consume(sa_ref, mxa_ref, n_far - 1)

    @pl.when(jnp.logical_and(qi >= 1, jnp.logical_not(last_in_a)))
    def _():
        next_diagonal()
        consume(sb_ref, mxb_ref, jnp.maximum(n_far, 1) - 1)

    lam = lam_ref[0, 0]
    o_t = acc_ref[0] * (1.0 / l_ref[0]) - lam * (acc_ref[1] * (1.0 / l_ref[1]))
    o = o_t.T
    o = _rms_norm_rows(o, sg_ref[...]) * (1.0 - lam_init)
    a_ref[0] = (o * _silu(ga_ref[0])).astype(_BF16)


def _prompt_attn(lam, qt, kb, vt, bias, ga, subln_g, lam_init):
    b, s, _ = kb.shape
    return pl.pallas_call(
        functools.partial(_prompt_attn_kernel, lam_init=lam_init),
        grid=(b, H_A, s // TQ),
        in_specs=[pl.BlockSpec(memory_space=pltpu.SMEM),
                  pl.BlockSpec((1, DK, s), lambda bi, h, qi: (bi, h, 0)),
                  pl.BlockSpec((1, s, DK), lambda bi, h, qi: (bi, 0, h)),
                  pl.BlockSpec((1, 1, VA, s), lambda bi, h, qi: (bi, h, 0, 0)),
                  pl.BlockSpec((1, 2, TK, TQ), lambda bi, h, qi: (h, 0, 0, 0)),
                  pl.BlockSpec((1, TQ, DK), lambda bi, h, qi: (bi, qi, h)),
                  pl.BlockSpec(subln_g.shape, lambda bi, h, qi: (0, 0))],
        out_specs=pl.BlockSpec((1, TQ, DK), lambda bi, h, qi: (bi, qi, h)),
        out_shape=jax.ShapeDtypeStruct((b, s, W_A), _BF16),
        scratch_shapes=[pltpu.VMEM((DK, 2 * TQ), _BF16),
                        pltpu.VMEM((DK, 2 * TQ), _BF16),
                        pltpu.VMEM((TK, 2 * TQ), _F32),
                        pltpu.VMEM((TK, 2 * TQ), _F32),
                        pltpu.VMEM((TK, 2 * TQ), _F32),
                        pltpu.VMEM((1, 2 * TQ), _F32),
                        pltpu.VMEM((1, 2 * TQ), _F32),
                        pltpu.VMEM((1, 2 * TQ), _F32),
                        pltpu.VMEM((2, 1, TQ), _F32),
                        pltpu.VMEM((2, 1, TQ), _F32),
                        pltpu.VMEM((2, DK, TQ), _F32)],
        compiler_params=pltpu.CompilerParams(dimension_semantics=("arbitrary", "arbitrary", "arbitrary"),
                                             vmem_limit_bytes=VMEM_LIMIT),
        name="prompt_attn",
    )(lam, qt, kb, vt, bias, ga, subln_g)


def _out_proj_kernel(a_ref, g_ref, x_ref, wa_ref, wg_ref, lng_ref, lnb_ref, y_ref, *, alpha):
    sub = min(OUT_SUB, a_ref.shape[0])
    for i in range(a_ref.shape[0] // sub):
        rows = slice(i * sub, (i + 1) * sub)
        out = (jnp.dot(a_ref[rows, :], wa_ref[...], preferred_element_type=_F32)
               + jnp.dot(g_ref[rows, :], wg_ref[...], preferred_element_type=_F32))
        z = alpha * x_ref[rows, :] + out
        y_ref[rows, :] = _layer_norm_rows(z, lng_ref[...], lnb_ref[...])


def _out_proj(a, g, x, wo_a, wo_g, ln_g, ln_b, alpha):
    n, d = x.shape
    tm = min(n, TM_OUT)
    row = lambda i: (i, 0)
    const = lambda i: (0, 0)
    return pl.pallas_call(
        functools.partial(_out_proj_kernel, alpha=alpha),
        grid=(n // tm,),
        in_specs=[pl.BlockSpec((tm, W_A), row), pl.BlockSpec((tm, W_B), row), pl.BlockSpec((tm, d), row),
                  pl.BlockSpec(wo_a.shape, const), pl.BlockSpec(wo_g.shape, const),
                  pl.BlockSpec(ln_g.shape, const), pl.BlockSpec(ln_b.shape, const)],
        out_specs=pl.BlockSpec((tm, d), row),
        out_shape=jax.ShapeDtypeStruct((n, d), _F32),
        compiler_params=pltpu.CompilerParams(dimension_semantics=("arbitrary",),
                                             vmem_limit_bytes=VMEM_LIMIT),
        name="out_proj",
    )(a, g, x, wo_a, wo_g, ln_g, ln_b)


def _sample_proj_kernel(x_ref, w_ref, lng_ref, lnb_ref, cs_ref, bs_ref, lp_ref,
                        q_ref, k_ref, v_ref, ga_ref, g_ref, vn_ref, lam_ref, *, lam_init, t_valid):
    rows = x_ref.shape[0]
    h = jnp.dot(x_ref[...].astype(_BF16), w_ref[...], preferred_element_type=_F32)
    q_ref[...] = h[:, 0:W_A]
    k_ref[...] = h[:, W_A:2 * W_A]
    v_ref[...] = h[:, 2 * W_A:3 * W_A]
    ga_ref[...] = h[:, 3 * W_A:4 * W_A]
    o = 4 * W_A
    u = h[:, o:o + W_B]
    vb = h[:, o + W_B:o + 2 * W_B]
    gb = h[:, o + 2 * W_B:o + 3 * W_B]
    vn = _layer_norm_rows(vb, lng_ref[...], lnb_ref[...])
    vn_ref[...] = vn
    vn3 = vn.reshape(rows // T_PAD, T_PAD, W_B)
    s3 = jnp.broadcast_to(bs_ref[...][None], vn3.shape)
    for j in range(t_valid):
        s3 = s3 + vn3[:, j:j + 1, :] * cs_ref[j][None]
    g_ref[...] = (u * s3.reshape(rows, W_B) * _silu(gb)).astype(_BF16)
    lp = lp_ref[...]
    e1 = jnp.exp(jnp.sum(lp[0:1] * lp[1:2], axis=-1, keepdims=True))
    e2 = jnp.exp(jnp.sum(lp[2:3] * lp[3:4], axis=-1, keepdims=True))
    lam_ref[...] = jnp.broadcast_to(e1 - e2 + lam_init, lam_ref.shape)


def _sample_proj(x, w, ln_g, ln_b, cs, bs8, lam_params, lam_init, t_valid):
    n, d = x.shape
    tm = min(n, 256)
    row = lambda i: (i, 0)
    const2 = lambda i: (0, 0)
    const3 = lambda i: (0, 0, 0)
    f32_rows = jax.ShapeDtypeStruct((n, W_A), _F32)
    return pl.pallas_call(
        functools.partial(_sample_proj_kernel, lam_init=lam_init, t_valid=t_valid),
        grid=(n // tm,),
        in_specs=[pl.BlockSpec((tm, d), row), pl.BlockSpec(w.shape, const2),
                  pl.BlockSpec(ln_g.shape, const2), pl.BlockSpec(ln_b.shape, const2),
                  pl.BlockSpec(cs.shape, const3), pl.BlockSpec(bs8.shape, const2),
                  pl.BlockSpec(lam_params.shape, const2)],
        out_specs=[pl.BlockSpec((tm, W_A), row)] * 4
                  + [pl.BlockSpec((tm, W_B), row), pl.BlockSpec((tm, W_B), row),
                     pl.BlockSpec((8, 128), const2)],
        out_shape=[f32_rows] * 4 + [jax.ShapeDtypeStruct((n, W_B), _BF16),
                                    jax.ShapeDtypeStruct((n, W_B), _F32),
                                    jax.ShapeDtypeStruct((8, 128), _F32)],
        compiler_params=pltpu.CompilerParams(dimension_semantics=("arbitrary",),
                                             vmem_limit_bytes=VMEM_LIMIT),
        name="sample_proj",
    )(x, w, ln_g, ln_b, cs, bs8, lam_params)


def _sample_attn_kernel(pt_ref, lam_ref, q_ref, kn_ref, vn_ref, ga_ref, hm_ref, neg_ref, bl_ref, bn_ref, sg_ref,
                        ck_ref, cv_ref, a_ref, kbuf, vbuf, ksem, vsem, qm_ref, s0_ref, s1_ref, m_ref, l_ref, acc_ref,
                        *, lam_init, n_groups, n_seq):
    gp = PAGES_PER_GROUP
    b = pl.program_id(0)
    n_rows = 2 * H_A * T_PAD
    page_rows = kbuf.shape[1]

    def group_copies(hbm_ref, buf, sem, seq, grp):
        slot = jnp.bitwise_and(grp, RING_GROUPS - 1)
        return [pltpu.make_async_copy(hbm_ref.at[pt_ref[seq, grp * gp + i]], buf.at[slot * gp + i], sem.at[slot])
                for i in range(gp)]

    def start_ahead(hbm_ref, buf, sem, grp):
        ahead = jnp.asarray(grp + RING_GROUPS, jnp.int32)
        wraps = (ahead >= n_groups).astype(jnp.int32)
        seq = b + wraps
        grp2 = ahead - wraps * n_groups

        @pl.when(seq < n_seq)
        def _():
            for i, c in enumerate(group_copies(hbm_ref, buf, sem, seq, grp2)):
                c.start(priority=i % 2)

    def wait_group(hbm_ref, buf, sem, grp):
        for c in group_copies(hbm_ref, buf, sem, b, grp):
            c.wait()

    @pl.when(b == 0)
    def _():
        for g in range(RING_GROUPS):
            for i, c in enumerate(group_copies(ck_ref, kbuf, ksem, 0, g) + group_copies(cv_ref, vbuf, vsem, 0, g)):
                c.start(priority=i % 2)

    q8 = q_ref[0]
    per_head = [q8[:, h * DK:(h + 1) * DK] for h in range(H_A)]
    qm_ref[...] = jnp.concatenate(per_head + per_head, axis=0) * hm_ref[...]
    m_ref[...] = jnp.full(m_ref.shape, NEG, _F32)
    l_ref[...] = jnp.zeros(l_ref.shape, _F32)
    acc_ref[...] = jnp.zeros(acc_ref.shape, _F32)

    def scores(grp, dst_ref, with_last_bias):
        slot = jnp.bitwise_and(grp, RING_GROUPS - 1)
        qm = qm_ref[...]
        for i in range(gp):
            s = lax.dot_general(qm, kbuf[slot * gp + i], _NT, preferred_element_type=_F32) + neg_ref[...]
            if with_last_bias and i == gp - 1:
                s = s + bl_ref[...]
            dst_ref[:, i * page_rows:(i + 1) * page_rows] = s

    def consume(grp, src_ref):
        slot = jnp.bitwise_and(grp, RING_GROUPS - 1)
        smax = src_ref[:, 0:page_rows]
        for i in range(1, gp):
            smax = jnp.maximum(smax, src_ref[:, i * page_rows:(i + 1) * page_rows])
        m_old = m_ref[...]
        m_new = jnp.maximum(m_old, jnp.max(smax, axis=-1, keepdims=True))
        alpha = jnp.exp2(m_old - m_new)
        l = alpha * l_ref[...]
        acc = alpha * acc_ref[...]
        for i in range(gp):
            p = jnp.exp2(src_ref[:, i * page_rows:(i + 1) * page_rows] - m_new)
            l = l + p
            acc = acc + jnp.dot(p, vbuf[slot * gp + i], preferred_element_type=_F32)
        l_ref[...] = l
        acc_ref[...] = acc
        m_ref[...] = m_new

    def block(grp, dst_ref, src_ref, with_last_bias=False):
        wait_group(ck_ref, kbuf, ksem, grp)
        wait_group(cv_ref, vbuf, vsem, grp - 1)
        scores(grp, dst_ref, with_last_bias)
        consume(grp - 1, src_ref)
        start_ahead(ck_ref, kbuf, ksem, grp)
        start_ahead(cv_ref, vbuf, vsem, grp - 1)

    wait_group(ck_ref, kbuf, ksem, 0)
    scores(0, s0_ref, False)
    start_ahead(ck_ref, kbuf, ksem, 0)

    def pair(i, carry):
        g = 2 * i + 1
        block(g, s1_ref, s0_ref)
        block(g + 1, s0_ref, s1_ref)
        return carry

    lax.fori_loop(0, (n_groups - 2) // 2, pair, 0)
    block(n_groups - 1, s1_ref, s0_ref, with_last_bias=True)
    wait_group(cv_ref, vbuf, vsem, n_groups - 1)
    consume(n_groups - 1, s1_ref)
    start_ahead(cv_ref, vbuf, vsem, n_groups - 1)

    sn = lax.dot_general(qm_ref[...], kn_ref[0], _NT, preferred_element_type=_F32) + bn_ref[...]
    m_o = m_ref[...]
    m_f = jnp.maximum(m_o, jnp.max(sn, axis=-1, keepdims=True))
    al = jnp.exp2(m_o - m_f)
    pn = jnp.exp2(sn - m_f)
    l_tot = al * jnp.sum(l_ref[...], axis=-1, keepdims=True) + jnp.sum(pn, axis=-1, keepdims=True)
    acc = al * acc_ref[...] + jnp.dot(pn, vn_ref[0], preferred_element_type=_F32)
    o_full = acc * (1.0 / l_tot)
    half = n_rows // 2
    od = o_full[0:half] - lam_ref[0, 0] * o_full[half:n_rows]
    ga = ga_ref[0]
    for h in range(H_A):
        cols = slice(h * DK, (h + 1) * DK)
        oh = _rms_norm_rows(od[h * T_PAD:(h + 1) * T_PAD], sg_ref[...]) * (1.0 - lam_init)
        a_ref[0, :, cols] = (oh * _silu(ga[:, cols])).astype(_BF16)


def _sample_attn(page_table, lam, q8, kn, vn, ga8, halfmask, negmask, bias_last, bias_new, subln_g, ck, cv,
                 lam_init):
    db, n_pages = page_table.shape
    gp = PAGES_PER_GROUP
    n_groups = n_pages // gp
    assert n_pages % gp == 0 and n_groups % RING_GROUPS == 0 and n_groups % 2 == 0
    page_rows = ck.shape[1]
    n_rows = 2 * H_A * T_PAD
    seq = lambda bi, pt: (bi, 0, 0)
    const = lambda bi, pt: (0, 0)
    grid_spec = pltpu.PrefetchScalarGridSpec(
        num_scalar_prefetch=1,
        grid=(db,),
        in_specs=[pl.BlockSpec(memory_space=pltpu.SMEM),
                  pl.BlockSpec((1, T_PAD, W_A), seq),
                  pl.BlockSpec((1, T_PAD * H_A, DK), seq), pl.BlockSpec((1, T_PAD * H_A, DK), seq),
                  pl.BlockSpec((1, T_PAD, W_A), seq),
                  pl.BlockSpec(halfmask.shape, const), pl.BlockSpec(negmask.shape, const),
                  pl.BlockSpec(bias_last.shape, const), pl.BlockSpec(bias_new.shape, const),
                  pl.BlockSpec(subln_g.shape, const),
                  pl.BlockSpec(memory_space=pl.ANY), pl.BlockSpec(memory_space=pl.ANY)],
        out_specs=pl.BlockSpec((1, T_PAD, W_A), seq),
        scratch_shapes=[pltpu.VMEM((RING_GROUPS * gp, page_rows, DK), _F32),
                        pltpu.VMEM((RING_GROUPS * gp, page_rows, DK), _F32),
                        pltpu.SemaphoreType.DMA((RING_GROUPS,)),
                        pltpu.SemaphoreType.DMA((RING_GROUPS,)),
                        pltpu.VMEM((n_rows, DK), _F32),
                        pltpu.VMEM((n_rows, gp * page_rows), _F32),
                        pltpu.VMEM((n_rows, gp * page_rows), _F32),
                        pltpu.VMEM((n_rows, 1), _F32),
                        pltpu.VMEM((n_rows, page_rows), _F32),
                        pltpu.VMEM((n_rows, DK), _F32)],
    )
    return pl.pallas_call(
        functools.partial(_sample_attn_kernel, lam_init=lam_init, n_groups=n_groups, n_seq=db),
        grid_spec=grid_spec,
        out_shape=jax.ShapeDtypeStruct((db, T_PAD, W_A), _BF16),
        compiler_params=pltpu.CompilerParams(dimension_semantics=("arbitrary",),
                                             vmem_limit_bytes=VMEM_LIMIT),
        name="sample_attn",
    )(page_table, lam, q8, kn, vn, ga8, halfmask, negmask, bias_last, bias_new, subln_g, ck, cv)


def kernel(x_prompt, x_sample, cache_k, cache_v, page_table, w_in, w_out, lambda_q1, lambda_k1, lambda_q2,
           lambda_k2, subln_g, rel_bias, sgu_ln_g, sgu_ln_b, sgu_w, sgu_b, post_ln_g, post_ln_b):
    depth = w_in.shape[0]
    assert depth == 1, "single trunk layer"
    b, s, d = x_prompt.shape
    db, t, _ = x_sample.shape
    n_pool, page = cache_k.shape[1], cache_k.shape[2]
    assert s % TQ == 0 and s % TM == 0 and TQ == TK and TQ >= MAX_DISTANCE
    assert t <= T_PAD and page >= MAX_DISTANCE
    lam_init = 0.8 - 0.6 * math.exp(-0.3 * 0)
    alpha = (2.0 * depth) ** 0.25

    col_scale = jnp.where(jnp.arange(w_in.shape[2]) < W_A, SCALE * LOG2E, 1.0).astype(_F32)
    w_all = (w_in[0] * col_scale[None, :]).astype(_BF16)
    wo_a = w_out[0, 0:W_A].astype(_BF16)
    wo_g = w_out[0, W_A:].astype(_BF16)
    ln_g = sgu_ln_g[0][None]
    ln_b = sgu_ln_b[0][None]
    pg_ = post_ln_g[0][None]
    pb_ = post_ln_b[0][None]
    sg = subln_g[0][None]
    tril = jnp.tril(jnp.ones((CHUNK, CHUNK), _F32))
    wt = (sgu_w[0] * tril).astype(_BF16)
    bsb = jnp.broadcast_to(sgu_b[0][:, :, None], (H_B, CHUNK, CB))
    wsmall = sgu_w[0][:, :T_PAD, :T_PAD] * tril[:T_PAD, :T_PAD]
    valid_t = (jnp.arange(T_PAD) < t).astype(_F32)
    cs = jnp.transpose(wsmall, (2, 1, 0)) * valid_t[None, :, None]
    cs = jnp.repeat(cs, CB, axis=2)
    bs8 = jnp.repeat(jnp.transpose(sgu_b[0][:, :T_PAD]) * valid_t[:, None], CB, axis=1)
    lam_params = jnp.concatenate([lambda_q1, lambda_k1, lambda_q2, lambda_k2], axis=0)

    kk = np.arange(TK)[:, None]
    qq = np.arange(TQ)[None, :]
    idx_sub = _bucket_np(TQ + qq - kk)
    idx_diag = np.where(qq >= kk, _bucket_np(qq - kk), MASKED_BUCKET)
    bias_prompt = _bias_tiles(rel_bias, jnp.asarray(np.stack([idx_sub, idx_diag]).astype(np.int32)))
    tt = np.arange(T_PAD)[:, None]
    ii = np.arange(page * H_A)[None, :] // H_A
    idx_last = _bucket_np(np.where(tt < t, page + tt - ii, MAX_DISTANCE))
    idx_new = np.where((ii <= tt) & (tt < t), _bucket_np(tt - ii), MASKED_BUCKET)
    bias_sample = _bias_tiles(rel_bias, jnp.asarray(np.stack([idx_last, idx_new]).astype(np.int32)))
    r = np.arange(2 * H_A * T_PAD)
    r_head = (r // T_PAD) % H_A
    r_map = r // (H_A * T_PAD)
    lane_head = np.arange(page * H_A) % H_A
    own_head = r_head[:, None] == lane_head[None, :]
    negmask = jnp.asarray(np.where(own_head, 0.0, NEG).astype(np.float32))
    halfmask = jnp.asarray((np.arange(DK)[None, :] // HD == r_map[:, None]).astype(np.float32))
    bias_last = jnp.tile(bias_sample[:, 0].reshape(H_A * T_PAD, page * H_A), (2, 1))
    bias_new = (jnp.tile(bias_sample[:, 1, :, :T_PAD * H_A].reshape(H_A * T_PAD, T_PAD * H_A), (2, 1))
                + negmask[:, :T_PAD * H_A])

    xs = jnp.pad(x_sample, ((0, 0), (0, T_PAD - t), (0, 0))).reshape(db * T_PAD, d)
    q8, k8, v8, ga8, g8, vn8, lam_tile = _sample_proj(xs, w_all, ln_g, ln_b, cs, bs8, lam_params, lam_init, t)
    lam = lam_tile[0:1, 0:1]
    ck = cache_k[0].reshape(n_pool, page * H_A, DK)
    cv = cache_v[0].reshape(n_pool, page * H_A, DK)
    a8 = _sample_attn(page_table, lam, q8.reshape(db, T_PAD, W_A), k8.reshape(db, T_PAD * H_A, DK),
                      v8.reshape(db, T_PAD * H_A, DK), ga8.reshape(db, T_PAD, W_A), halfmask, negmask,
                      bias_last, bias_new, sg, ck, cv, lam_init)
    ys = _out_proj(a8.reshape(db * T_PAD, W_A), g8, xs, wo_a, wo_g, pg_, pb_, alpha)
    y_sample = ys.reshape(db, T_PAD, d)[:, :t]
    nks = k8.reshape(db, T_PAD, H_A, DK)[None, :, :t]
    nvs = v8.reshape(db, T_PAD, H_A, DK)[None, :, :t]
    nsv = vn8.reshape(db, T_PAD, H_B, CB)[None, :, :t]

    kf, vf, kb, qt, vt, ga, g = _prompt_proj(x_prompt, w_all, ln_g, ln_b, wt, bsb)
    a = _prompt_attn(lam, qt, kb, vt, bias_prompt, ga, sg, lam_init)
    yp = _out_proj(a.reshape(b * s, W_A), g.reshape(b * s, W_B), x_prompt.reshape(b * s, d),
                   wo_a, wo_g, pg_, pb_, alpha)
    y_prompt = yp.reshape(b, s, d)
    nkp = kf.reshape(1, b, s, H_A, DK)
    nvp = vf.reshape(1, b, s, H_A, DK)
    return (y_prompt, y_sample, nkp, nvp, nks, nvs, nsv)
```

```python
import functools
import math

import numpy as np
import jax
import jax.numpy as jnp
from jax import lax
from jax.experimental import pallas as pl
from jax.experimental.pallas import tpu as pltpu

H_A = 4
HD = 64
DK = 2 * HD
W_A = H_A * DK
H_B = 4
CB = 128
W_B = H_B * CB
CHUNK = 128
N_BUCKETS = 32
MAX_DISTANCE = 128
LN_EPS = 1e-5
SCALE = HD ** -0.5
LOG2E = math.log2(math.e)
NEG = -1e30
MASKED_BUCKET = N_BUCKETS

TM = 512
TM_OUT = 1024
OUT_SUB = 256
TQ = 512
TK = 512
QH = TQ // 2
ONES_ROWS = 16
VA = DK + ONES_ROWS
T_PAD = 8
PAGES_PER_GROUP = 8
RING_GROUPS = 8
VMEM_LIMIT = 56 * 1024 * 1024

_F32 = jnp.float32
_BF16 = jnp.bfloat16
_NT = (((1,), (1,)), ((), ()))


def _bucket_np(dist):
    n = np.maximum(dist, 0)
    max_exact = N_BUCKETS // 2
    nf = np.maximum(n, 1).astype(np.float32)
    large = max_exact + (np.log(nf / max_exact) / math.log(MAX_DISTANCE / max_exact)
                         * (N_BUCKETS - max_exact)).astype(np.int32)
    large = np.minimum(large, N_BUCKETS - 1)
    return np.where(n < max_exact, n, large).astype(np.int32)


def _silu(x):
    return x * jax.nn.sigmoid(x)


def _layer_norm_rows(x, g, b):
    mu = jnp.mean(x, axis=-1, keepdims=True)
    xc = x - mu
    var = jnp.mean(xc * xc, axis=-1, keepdims=True)
    return xc * lax.rsqrt(var + LN_EPS) * g + b


def _rms_norm_rows(x, g):
    return x * lax.rsqrt(jnp.mean(x * x, axis=-1, keepdims=True) + LN_EPS) * g


def _bias_kernel(tab_ref, idx_ref, o_ref):
    h = pl.program_id(0)
    idx = idx_ref[0]
    far = tab_ref[N_BUCKETS - 1, h]
    acc = jnp.full(idx.shape, NEG, _F32)
    for b in range(N_BUCKETS):
        acc = jnp.where(idx == b, (tab_ref[b, h] - far) * LOG2E, acc)
    o_ref[0, 0] = acc


def _bias_tiles(table, idx):
    n, r, c = idx.shape
    return pl.pallas_call(
        _bias_kernel,
        grid=(H_A, n),
        in_specs=[pl.BlockSpec(memory_space=pltpu.SMEM),
                  pl.BlockSpec((1, r, c), lambda h, i: (i, 0, 0))],
        out_specs=pl.BlockSpec((1, 1, r, c), lambda h, i: (h, i, 0, 0)),
        out_shape=jax.ShapeDtypeStruct((H_A, n, r, c), _F32),
        name="bias_tiles",
    )(table, idx)


def _prompt_proj_kernel(x_ref, w_ref, lng_ref, lnb_ref, wt_ref, bs_ref,
                        kf_ref, vf_ref, kb_ref, qt_ref, vt_ref, ga_ref, g_ref):
    xb = x_ref[0].astype(_BF16)
    hm = jnp.dot(xb, w_ref[...], preferred_element_type=_F32)
    k = hm[:, W_A:2 * W_A]
    v = hm[:, 2 * W_A:3 * W_A]
    for h in range(H_A):
        cols = slice(h * DK, (h + 1) * DK)
        kf_ref[0, pl.ds(h, TM, stride=H_A), :] = k[:, cols]
        vf_ref[0, pl.ds(h, TM, stride=H_A), :] = v[:, cols]
    kb_ref[0] = k.astype(_BF16)
    ga_ref[0] = hm[:, 3 * W_A:4 * W_A]
    qt_ref[0] = hm[:, 0:W_A].T.astype(_BF16)
    vt = v.T.astype(_BF16)
    for h in range(H_A):
        vt_ref[0, h, 0:DK, :] = vt[h * DK:(h + 1) * DK]
        vt_ref[0, h, DK:VA, :] = jnp.ones((ONES_ROWS, TM), _BF16)
    o = 4 * W_A
    u = hm[:, o:o + W_B]
    vb = hm[:, o + W_B:o + 2 * W_B]
    gb = hm[:, o + 2 * W_B:o + 3 * W_B]
    vn = _layer_norm_rows(vb, lng_ref[...], lnb_ref[...]).astype(_BF16)
    gate = u * _silu(gb)
    n_chunks = TM // CHUNK
    for h in range(H_B):
        cols = slice(h * CB, (h + 1) * CB)
        vn_h = jnp.concatenate([vn[c * CHUNK:(c + 1) * CHUNK, cols] for c in range(n_chunks)], axis=1)
        s = jnp.dot(wt_ref[h], vn_h, preferred_element_type=_F32)
        for c in range(n_chunks):
            rows = slice(c * CHUNK, (c + 1) * CHUNK)
            g_ref[0, rows, cols] = (gate[rows, cols] * (s[:, c * CB:(c + 1) * CB] + bs_ref[h])).astype(_BF16)


def _prompt_proj(x, w_all, ln_g, ln_b, wt, bsb):
    b, s, d = x.shape
    row = lambda bi, si: (bi, si, 0)
    col = lambda bi, si: (bi, 0, si)
    const2 = lambda bi, si: (0, 0)
    const3 = lambda bi, si: (0, 0, 0)
    f32_heads = jax.ShapeDtypeStruct((b, s * H_A, DK), _F32)
    bf_rows = jax.ShapeDtypeStruct((b, s, W_A), _BF16)
    bf_cols = jax.ShapeDtypeStruct((b, W_A, s), _BF16)
    return pl.pallas_call(
        _prompt_proj_kernel,
        grid=(b, s // TM),
        in_specs=[pl.BlockSpec((1, TM, d), row),
                  pl.BlockSpec(w_all.shape, const2),
                  pl.BlockSpec(ln_g.shape, const2),
                  pl.BlockSpec(ln_b.shape, const2),
                  pl.BlockSpec(wt.shape, const3),
                  pl.BlockSpec(bsb.shape, const3)],
        out_specs=[pl.BlockSpec((1, TM * H_A, DK), row), pl.BlockSpec((1, TM * H_A, DK), row),
                   pl.BlockSpec((1, TM, W_A), row),
                   pl.BlockSpec((1, W_A, TM), col),
                   pl.BlockSpec((1, H_A, VA, TM), lambda bi, si: (bi, 0, 0, si)),
                   pl.BlockSpec((1, TM, W_A), row), pl.BlockSpec((1, TM, W_B), row)],
        out_shape=[f32_heads, f32_heads, bf_rows, bf_cols,
                   jax.ShapeDtypeStruct((b, H_A, VA, s), _BF16),
                   jax.ShapeDtypeStruct((b, s, W_A), _F32), jax.ShapeDtypeStruct((b, s, W_B), _BF16)],
        compiler_params=pltpu.CompilerParams(dimension_semantics=("arbitrary", "arbitrary"),
                                             vmem_limit_bytes=VMEM_LIMIT),
        name="prompt_proj",
    )(x, w_all, ln_g, ln_b, wt, bsb)


def _prompt_attn_kernel(lam_ref, qt_ref, k_ref, vt_ref, bias_ref, ga_ref, sg_ref, a_ref,
                        qz_ref, qzn_ref, sa_ref, sb_ref, sc_ref, mxa_ref, mxb_ref, mxc_ref, m_ref, l_ref, acc_ref,
                        *, lam_init):
    qi = pl.program_id(2)
    n_q = pl.num_programs(2)

    def split_maps(qt, dst_ref):
        row = lax.broadcasted_iota(jnp.int32, qt.shape, 0)
        zero = jnp.zeros_like(qt)
        dst_ref[:, 0:TQ] = jnp.where(row < HD, qt, zero)
        dst_ref[:, TQ:2 * TQ] = jnp.where(row >= HD, qt, zero)

    def begin():
        q_start = pl.multiple_of(qi * TQ, TQ)
        split_maps(qt_ref[0, :, pl.ds(q_start, TQ)], qz_ref)
        m_ref[...] = jnp.full(m_ref.shape, NEG, _F32)
        l_ref[...] = jnp.zeros(l_ref.shape, _F32)
        acc_ref[...] = jnp.zeros(acc_ref.shape, _F32)

    def finish():
        lam = lam_ref[0, 0]
        o_t = acc_ref[0] * (1.0 / l_ref[0]) - lam * (acc_ref[1] * (1.0 / l_ref[1]))
        o = o_t.T
        o = _rms_norm_rows(o, sg_ref[...]) * (1.0 - lam_init)
        a_ref[0] = (o * _silu(ga_ref[0])).astype(_BF16)

    def scores(j, dst_ref, mx_ref, bias_idx=None, q_ref=qz_ref):
        start = pl.multiple_of(j * TK, TK)
        s = jnp.dot(k_ref[0, pl.ds(start, TK), :], q_ref[...], preferred_element_type=_F32)
        if bias_idx == 1:
            bias = bias_ref[0, 1]
            s = jnp.concatenate([s[:, 0:TQ] + bias, s[:, TQ:2 * TQ] + bias], axis=1)
        elif bias_idx == 0:
            md = MAX_DISTANCE
            corner = bias_ref[0, 0, TK - md:TK, 0:md]
            low = s[TK - md:TK]
            low = jnp.concatenate([low[:, 0:md] + corner, low[:, md:TQ],
                                   low[:, TQ:TQ + md] + corner, low[:, TQ + md:2 * TQ]], axis=1)
            s = jnp.concatenate([s[0:TK - md], low], axis=0)
        dst_ref[...] = s
        mx_ref[...] = jnp.max(s, axis=0, keepdims=True)

    def consume(src_ref, mx_ref, j):
        start = pl.multiple_of(j * TK, TK)
        vblk = vt_ref[0, 0, :, pl.ds(start, TK)]
        for mp in range(2):
            for hf in range(TQ // QH):
                qs = slice(hf * QH, (hf + 1) * QH)
                cols = slice(mp * TQ + hf * QH, mp * TQ + (hf + 1) * QH)
                m_old = m_ref[mp, :, qs]
                m_new = jnp.maximum(m_old, mx_ref[:, cols])
                alpha = jnp.exp2(m_old - m_new)
                p = jnp.exp2(src_ref[:, cols] - m_new).astype(_BF16)
                pv = jnp.dot(vblk, p, preferred_element_type=_F32)
                acc_ref[mp, :, qs] = acc_ref[mp, :, qs] * alpha + pv[0:DK]
                l_ref[mp, :, qs] = l_ref[mp, :, qs] * alpha + pv[DK:DK + 1]
                m_ref[mp, :, qs] = m_new

    def next_diagonal():
        jn = jnp.minimum(qi + 1, n_q - 1)
        split_maps(qt_ref[0, :, pl.ds(pl.multiple_of(jn * TQ, TQ), TQ)], qzn_ref)
        scores(jn, sc_ref, mxc_ref, bias_idx=1, q_ref=qzn_ref)

    n_far = jnp.maximum(qi - 1, 0)
    n_rest = jnp.maximum(n_far - 1, 0)
    odd_rest = jnp.bitwise_and(n_rest, 1)

    @pl.when(qi == 0)
    def _():
        begin()
        scores(0, sc_ref, mxc_ref, bias_idx=1)
        consume(sc_ref, mxc_ref, 0)
        finish()

    @pl.when(qi == 1)
    def _():
        begin()
        scores(1, sc_ref, mxc_ref, bias_idx=1)
        scores(0, sb_ref, mxb_ref, bias_idx=0)
        consume(sc_ref, mxc_ref, 1)
        next_diagonal()
        consume(sb_ref, mxb_ref, 0)
        finish()

    @pl.when(qi >= 2)
    def _():
        begin()
        scores(qi - 1, sb_ref, mxb_ref, bias_idx=0)
        consume(sc_ref, mxc_ref, qi)
        scores(0, sa_ref, mxa_ref)
        consume(sb_ref, mxb_ref, qi - 1)

    def far_tiles(f0, count):
        for t in range(0, count, 2):
            scores(f0 + t + 1, sb_ref, mxb_ref)
            consume(sa_ref, mxa_ref, f0 + t)
            scores(f0 + t + 2, sa_ref, mxa_ref)
            consume(sb_ref, mxb_ref, f0 + t + 1)

    def far_quad(i, carry):
        far_tiles(4 * i, 4)
        return carry

    n_quads = lax.shift_right_logical(n_rest, 2)
    lax.fori_loop(0, n_quads, far_quad, 0)

    @pl.when(jnp.bitwise_and(n_rest, 2) == 2)
    def _():
        far_tiles(4 * n_quads, 2)

    @pl.when(jnp.logical_and(qi >= 2, odd_rest == 0))
    def _():
        next_diagonal()
        consume(sa_ref, mxa_ref, n_far - 1)
        finish()

    @pl.when(jnp.logical_and(qi >= 2, odd_rest == 1))
    def _():
        scores(n_far - 1, sb_ref, mxb_ref)
        consume(sa_ref, mxa_ref, n_far - 2)
        next_diagonal()
        consume(sb_ref, mxb_ref, n_far - 1)
        finish()


def _prompt_attn(lam, qt, kb, vt, bias, ga, subln_g, lam_init):
    b, s, _ = kb.shape
    return pl.pallas_call(
        functools.partial(_prompt_attn_kernel, lam_init=lam_init),
        grid=(b, H_A, s // TQ),
        in_specs=[pl.BlockSpec(memory_space=pltpu.SMEM),
                  pl.BlockSpec((1, DK, s), lambda bi, h, qi: (bi, h, 0)),
                  pl.BlockSpec((1, s, DK), lambda bi, h, qi: (bi, 0, h)),
                  pl.BlockSpec((1, 1, VA, s), lambda bi, h, qi: (bi, h, 0, 0)),
                  pl.BlockSpec((1, 2, TK, TQ), lambda bi, h, qi: (h, 0, 0, 0)),
                  pl.BlockSpec((1, TQ, DK), lambda bi, h, qi: (bi, qi, h)),
                  pl.BlockSpec(subln_g.shape, lambda bi, h, qi: (0, 0))],
        out_specs=pl.BlockSpec((1, TQ, DK), lambda bi, h, qi: (bi, qi, h)),
        out_shape=jax.ShapeDtypeStruct((b, s, W_A), _BF16),
        scratch_shapes=[pltpu.VMEM((DK, 2 * TQ), _BF16),
                        pltpu.VMEM((DK, 2 * TQ), _BF16),
                        pltpu.VMEM((TK, 2 * TQ), _F32),
                        pltpu.VMEM((TK, 2 * TQ), _F32),
                        pltpu.VMEM((TK, 2 * TQ), _F32),
                        pltpu.VMEM((1, 2 * TQ), _F32),
                        pltpu.VMEM((1, 2 * TQ), _F32),
                        pltpu.VMEM((1, 2 * TQ), _F32),
                        pltpu.VMEM((2, 1, TQ), _F32),
                        pltpu.VMEM((2, 1, TQ), _F32),
                        pltpu.VMEM((2, DK, TQ), _F32)],
        compiler_params=pltpu.CompilerParams(dimension_semantics=("arbitrary", "arbitrary", "arbitrary"),
                                             vmem_limit_bytes=VMEM_LIMIT),
        name="prompt_attn",
    )(lam, qt, kb, vt, bias, ga, subln_g)


def _out_proj_kernel(a_ref, g_ref, x_ref, wa_ref, wg_ref, lng_ref, lnb_ref, y_ref, *, alpha):
    sub = min(OUT_SUB, a_ref.shape[0])
    for i in range(a_ref.shape[0] // sub):
        rows = slice(i * sub, (i + 1) * sub)
        out = (jnp.dot(a_ref[rows, :], wa_ref[...], preferred_element_type=_F32)
               + jnp.dot(g_ref[rows, :], wg_ref[...], preferred_element_type=_F32))
        z = alpha * x_ref[rows, :] + out
        y_ref[rows, :] = _layer_norm_rows(z, lng_ref[...], lnb_ref[...])


def _out_proj(a, g, x, wo_a, wo_g, ln_g, ln_b, alpha):
    n, d = x.shape
    tm = min(n, TM_OUT)
    row = lambda i: (i, 0)
    const = lambda i: (0, 0)
    return pl.pallas_call(
        functools.partial(_out_proj_kernel, alpha=alpha),
        grid=(n // tm,),
        in_specs=[pl.BlockSpec((tm, W_A), row), pl.BlockSpec((tm, W_B), row), pl.BlockSpec((tm, d), row),
                  pl.BlockSpec(wo_a.shape, const), pl.BlockSpec(wo_g.shape, const),
                  pl.BlockSpec(ln_g.shape, const), pl.BlockSpec(ln_b.shape, const)],
        out_specs=pl.BlockSpec((tm, d), row),
        out_shape=jax.ShapeDtypeStruct((n, d), _F32),
        compiler_params=pltpu.CompilerParams(dimension_semantics=("arbitrary",),
                                             vmem_limit_bytes=VMEM_LIMIT),
        name="out_proj",
    )(a, g, x, wo_a, wo_g, ln_g, ln_b)


def _sample_proj_kernel(x_ref, w_ref, lng_ref, lnb_ref, cs_ref, bs_ref, lp_ref,
                        q_ref, k_ref, v_ref, ga_ref, g_ref, vn_ref, lam_ref, *, lam_init, t_valid):
    rows = x_ref.shape[0]
    h = jnp.dot(x_ref[...].astype(_BF16), w_ref[...], preferred_element_type=_F32)
    q_ref[...] = h[:, 0:W_A]
    k_ref[...] = h[:, W_A:2 * W_A]
    v_ref[...] = h[:, 2 * W_A:3 * W_A]
    ga_ref[...] = h[:, 3 * W_A:4 * W_A]
    o = 4 * W_A
    u = h[:, o:o + W_B]
    vb = h[:, o + W_B:o + 2 * W_B]
    gb = h[:, o + 2 * W_B:o + 3 * W_B]
    vn = _layer_norm_rows(vb, lng_ref[...], lnb_ref[...])
    vn_ref[...] = vn
    vn3 = vn.reshape(rows // T_PAD, T_PAD, W_B)
    s3 = jnp.broadcast_to(bs_ref[...][None], vn3.shape)
    for j in range(t_valid):
        s3 = s3 + vn3[:, j:j + 1, :] * cs_ref[j][None]
    g_ref[...] = (u * s3.reshape(rows, W_B) * _silu(gb)).astype(_BF16)
    lp = lp_ref[...]
    e1 = jnp.exp(jnp.sum(lp[0:1] * lp[1:2], axis=-1, keepdims=True))
    e2 = jnp.exp(jnp.sum(lp[2:3] * lp[3:4], axis=-1, keepdims=True))
    lam_ref[...] = jnp.broadcast_to(e1 - e2 + lam_init, lam_ref.shape)


def _sample_proj(x, w, ln_g, ln_b, cs, bs8, lam_params, lam_init, t_valid):
    n, d = x.shape
    tm = min(n, 256)
    row = lambda i: (i, 0)
    const2 = lambda i: (0, 0)
    const3 = lambda i: (0, 0, 0)
    f32_rows = jax.ShapeDtypeStruct((n, W_A), _F32)
    return pl.pallas_call(
        functools.partial(_sample_proj_kernel, lam_init=lam_init, t_valid=t_valid),
        grid=(n // tm,),
        in_specs=[pl.BlockSpec((tm, d), row), pl.BlockSpec(w.shape, const2),
                  pl.BlockSpec(ln_g.shape, const2), pl.BlockSpec(ln_b.shape, const2),
                  pl.BlockSpec(cs.shape, const3), pl.BlockSpec(bs8.shape, const2),
                  pl.BlockSpec(lam_params.shape, const2)],
        out_specs=[pl.BlockSpec((tm, W_A), row)] * 4
                  + [pl.BlockSpec((tm, W_B), row), pl.BlockSpec((tm, W_B), row),
                     pl.BlockSpec((8, 128), const2)],
        out_shape=[f32_rows] * 4 + [jax.ShapeDtypeStruct((n, W_B), _BF16),
                                    jax.ShapeDtypeStruct((n, W_B), _F32),
                                    jax.ShapeDtypeStruct((8, 128), _F32)],
        compiler_params=pltpu.CompilerParams(dimension_semantics=("arbitrary",),
                                             vmem_limit_bytes=VMEM_LIMIT),
        name="sample_proj",
    )(x, w, ln_g, ln_b, cs, bs8, lam_params)


def _sample_attn_kernel(pt_ref, lam_ref, q_ref, kn_ref, vn_ref, ga_ref, hm_ref, neg_ref, bl_ref, bn_ref, sg_ref,
                        ck_ref, cv_ref, a_ref, kbuf, vbuf, ksem, vsem, qm_ref, s0_ref, s1_ref, m_ref, l_ref, acc_ref,
                        *, lam_init, n_groups, n_seq):
    gp = PAGES_PER_GROUP
    b = pl.program_id(0)
    n_rows = 2 * H_A * T_PAD
    page_rows = kbuf.shape[1]

    def group_copies(hbm_ref, buf, sem, seq, grp):
        slot = jnp.bitwise_and(grp, RING_GROUPS - 1)
        return [pltpu.make_async_copy(hbm_ref.at[pt_ref[seq, grp * gp + i]], buf.at[slot * gp + i], sem.at[slot])
                for i in range(gp)]

    def start_ahead(hbm_ref, buf, sem, grp):
        ahead = jnp.asarray(grp + RING_GROUPS, jnp.int32)
        wraps = (ahead >= n_groups).astype(jnp.int32)
        seq = b + wraps
        grp2 = ahead - wraps * n_groups

        @pl.when(seq < n_seq)
        def _():
            for i, c in enumerate(group_copies(hbm_ref, buf, sem, seq, grp2)):
                c.start(priority=i % 2)

    def wait_group(hbm_ref, buf, sem, grp):
        for c in group_copies(hbm_ref, buf, sem, b, grp):
            c.wait()

    @pl.when(b == 0)
    def _():
        for g in range(RING_GROUPS):
            for i, c in enumerate(group_copies(ck_ref, kbuf, ksem, 0, g) + group_copies(cv_ref, vbuf, vsem, 0, g)):
                c.start(priority=i % 2)

    q8 = q_ref[0]
    per_head = [q8[:, h * DK:(h + 1) * DK] for h in range(H_A)]
    qm_ref[...] = jnp.concatenate(per_head + per_head, axis=0) * hm_ref[...]
    m_ref[...] = jnp.full(m_ref.shape, NEG, _F32)
    l_ref[...] = jnp.zeros(l_ref.shape, _F32)
    acc_ref[...] = jnp.zeros(acc_ref.shape, _F32)

    def scores(grp, dst_ref, with_last_bias):
        slot = jnp.bitwise_and(grp, RING_GROUPS - 1)
        qm = qm_ref[...]
        for i in range(gp):
            s = lax.dot_general(qm, kbuf[slot * gp + i], _NT, preferred_element_type=_F32) + neg_ref[...]
            if with_last_bias and i == gp - 1:
                s = s + bl_ref[...]
            dst_ref[:, i * page_rows:(i + 1) * page_rows] = s

    def consume(grp, src_ref):
        slot = jnp.bitwise_and(grp, RING_GROUPS - 1)
        smax = src_ref[:, 0:page_rows]
        for i in range(1, gp):
            smax = jnp.maximum(smax, src_ref[:, i * page_rows:(i + 1) * page_rows])
        m_old = m_ref[...]
        m_new = jnp.maximum(m_old, jnp.max(smax, axis=-1, keepdims=True))
        alpha = jnp.exp2(m_old - m_new)
        l = alpha * l_ref[...]
        acc = alpha * acc_ref[...]
        for i in range(gp):
            p = jnp.exp2(src_ref[:, i * page_rows:(i + 1) * page_rows] - m_new)
            l = l + p
            acc = acc + jnp.dot(p, vbuf[slot * gp + i], preferred_element_type=_F32)
        l_ref[...] = l
        acc_ref[...] = acc
        m_ref[...] = m_new

    def block(grp, dst_ref, src_ref, with_last_bias=False):
        wait_group(ck_ref, kbuf, ksem, grp)
        wait_group(cv_ref, vbuf, vsem, grp - 1)
        scores(grp, dst_ref, with_last_bias)
        consume(grp - 1, src_ref)
        start_ahead(ck_ref, kbuf, ksem, grp)
        start_ahead(cv_ref, vbuf, vsem, grp - 1)

    wait_group(ck_ref, kbuf, ksem, 0)
    scores(0, s0_ref, False)
    start_ahead(ck_ref, kbuf, ksem, 0)

    def pair(i, carry):
        g = 2 * i + 1
        block(g, s1_ref, s0_ref)
        block(g + 1, s0_ref, s1_ref)
        return carry

    lax.fori_loop(0, (n_groups - 2) // 2, pair, 0)
    block(n_groups - 1, s1_ref, s0_ref, with_last_bias=True)
    wait_group(cv_ref, vbuf, vsem, n_groups - 1)
    consume(n_groups - 1, s1_ref)
    start_ahead(cv_ref, vbuf, vsem, n_groups - 1)

    sn = lax.dot_general(qm_ref[...], kn_ref[0], _NT, preferred_element_type=_F32) + bn_ref[...]
    m_o = m_ref[...]
    m_f = jnp.maximum(m_o, jnp.max(sn, axis=-1, keepdims=True))
    al = jnp.exp2(m_o - m_f)
    pn = jnp.exp2(sn - m_f)
    l_tot = al * jnp.sum(l_ref[...], axis=-1, keepdims=True) + jnp.sum(pn, axis=-1, keepdims=True)
    acc = al * acc_ref[...] + jnp.dot(pn, vn_ref[0], preferred_element_type=_F32)
    o_full = acc * (1.0 / l_tot)
    half = n_rows // 2
    od = o_full[0:half] - lam_ref[0, 0] * o_full[half:n_rows]
    ga = ga_ref[0]
    for h in range(H_A):
        cols = slice(h * DK, (h + 1) * DK)
        oh = _rms_norm_rows(od[h * T_PAD:(h + 1) * T_PAD], sg_ref[...]) * (1.0 - lam_init)
        a_ref[0, :, cols] = (oh * _silu(ga[:, cols])).astype(_BF16)


def _sample_attn(page_table, lam, q8, kn, vn, ga8, halfmask, negmask, bias_last, bias_new, subln_g, ck, cv,
                 lam_init):
    db, n_pages = page_table.shape
    gp = PAGES_PER_GROUP
    n_groups = n_pages // gp
    assert n_pages % gp == 0 and n_groups % RING_GROUPS == 0 and n_groups % 2 == 0
    page_rows = ck.shape[1]
    n_rows = 2 * H_A * T_PAD
    seq = lambda bi, pt: (bi, 0, 0)
    const = lambda bi, pt: (0, 0)
    grid_spec = pltpu.PrefetchScalarGridSpec(
        num_scalar_prefetch=1,
        grid=(db,),
        in_specs=[pl.BlockSpec(memory_space=pltpu.SMEM),
                  pl.BlockSpec((1, T_PAD, W_A), seq),
                  pl.BlockSpec((1, T_PAD * H_A, DK), seq), pl.BlockSpec((1, T_PAD * H_A, DK), seq),
                  pl.BlockSpec((1, T_PAD, W_A), seq),
                  pl.BlockSpec(halfmask.shape, const), pl.BlockSpec(negmask.shape, const),
                  pl.BlockSpec(bias_last.shape, const), pl.BlockSpec(bias_new.shape, const),
                  pl.BlockSpec(subln_g.shape, const),
                  pl.BlockSpec(memory_space=pl.ANY), pl.BlockSpec(memory_space=pl.ANY)],
        out_specs=pl.BlockSpec((1, T_PAD, W_A), seq),
        scratch_shapes=[pltpu.VMEM((RING_GROUPS * gp, page_rows, DK), _F32),
                        pltpu.VMEM((RING_GROUPS * gp, page_rows, DK), _F32),
                        pltpu.SemaphoreType.DMA((RING_GROUPS,)),
                        pltpu.SemaphoreType.DMA((RING_GROUPS,)),
                        pltpu.VMEM((n_rows, DK), _F32),
                        pltpu.VMEM((n_rows, gp * page_rows), _F32),
                        pltpu.VMEM((n_rows, gp * page_rows), _F32),
                        pltpu.VMEM((n_rows, 1), _F32),
                        pltpu.VMEM((n_rows, page_rows), _F32),
                        pltpu.VMEM((n_rows, DK), _F32)],
    )
    return pl.pallas_call(
        functools.partial(_sample_attn_kernel, lam_init=lam_init, n_groups=n_groups, n_seq=db),
        grid_spec=grid_spec,
        out_shape=jax.ShapeDtypeStruct((db, T_PAD, W_A), _BF16),
        compiler_params=pltpu.CompilerParams(dimension_semantics=("arbitrary",),
                                             vmem_limit_bytes=VMEM_LIMIT),
        name="sample_attn",
    )(page_table, lam, q8, kn, vn, ga8, halfmask, negmask, bias_last, bias_new, subln_g, ck, cv)


def kernel(x_prompt, x_sample, cache_k, cache_v, page_table, w_in, w_out, lambda_q1, lambda_k1, lambda_q2,
           lambda_k2, subln_g, rel_bias, sgu_ln_g, sgu_ln_b, sgu_w, sgu_b, post_ln_g, post_ln_b):
    depth = w_in.shape[0]
    assert depth == 1, "single trunk layer"
    b, s, d = x_prompt.shape
    db, t, _ = x_sample.shape
    n_pool, page = cache_k.shape[1], cache_k.shape[2]
    assert s % TQ == 0 and s % TM == 0 and TQ == TK and TQ >= MAX_DISTANCE
    assert t <= T_PAD and page >= MAX_DISTANCE
    lam_init = 0.8 - 0.6 * math.exp(-0.3 * 0)
    alpha = (2.0 * depth) ** 0.25

    col_scale = jnp.where(jnp.arange(w_in.shape[2]) < W_A, SCALE * LOG2E, 1.0).astype(_F32)
    w_all = (w_in[0] * col_scale[None, :]).astype(_BF16)
    wo_a = w_out[0, 0:W_A].astype(_BF16)
    wo_g = w_out[0, W_A:].astype(_BF16)
    ln_g = sgu_ln_g[0][None]
    ln_b = sgu_ln_b[0][None]
    pg_ = post_ln_g[0][None]
    pb_ = post_ln_b[0][None]
    sg = subln_g[0][None]
    tril = jnp.tril(jnp.ones((CHUNK, CHUNK), _F32))
    wt = (sgu_w[0] * tril).astype(_BF16)
    bsb = jnp.broadcast_to(sgu_b[0][:, :, None], (H_B, CHUNK, CB))
    wsmall = sgu_w[0][:, :T_PAD, :T_PAD] * tril[:T_PAD, :T_PAD]
    valid_t = (jnp.arange(T_PAD) < t).astype(_F32)
    cs = jnp.transpose(wsmall, (2, 1, 0)) * valid_t[None, :, None]
    cs = jnp.repeat(cs, CB, axis=2)
    bs8 = jnp.repeat(jnp.transpose(sgu_b[0][:, :T_PAD]) * valid_t[:, None], CB, axis=1)
    lam_params = jnp.concatenate([lambda_q1, lambda_k1, lambda_q2, lambda_k2], axis=0)

    kk = np.arange(TK)[:, None]
    qq = np.arange(TQ)[None, :]
    idx_sub = _bucket_np(TQ + qq - kk)
    idx_diag = np.where(qq >= kk, _bucket_np(qq - kk), MASKED_BUCKET)
    bias_prompt = _bias_tiles(rel_bias, jnp.asarray(np.stack([idx_sub, idx_diag]).astype(np.int32)))
    tt = np.arange(T_PAD)[:, None]
    ii = np.arange(page * H_A)[None, :] // H_A
    idx_last = _bucket_np(np.where(tt < t, page + tt - ii, MAX_DISTANCE))
    idx_new = np.where((ii <= tt) & (tt < t), _bucket_np(tt - ii), MASKED_BUCKET)
    bias_sample = _bias_tiles(rel_bias, jnp.asarray(np.stack([idx_last, idx_new]).astype(np.int32)))
    r = np.arange(2 * H_A * T_PAD)
    r_head = (r // T_PAD) % H_A
    r_map = r // (H_A * T_PAD)
    lane_head = np.arange(page * H_A) % H_A
    own_head = r_head[:, None] == lane_head[None, :]
    negmask = jnp.asarray(np.where(own_head, 0.0, NEG).astype(np.float32))
    halfmask = jnp.asarray((np.arange(DK)[None, :] // HD == r_map[:, None]).astype(np.float32))
    bias_last = jnp.tile(bias_sample[:, 0].reshape(H_A * T_PAD, page * H_A), (2, 1))
    bias_new = (jnp.tile(bias_sample[:, 1, :, :T_PAD * H_A].reshape(H_A * T_PAD, T_PAD * H_A), (2, 1))
                + negmask[:, :T_PAD * H_A])

    xs = jnp.pad(x_sample, ((0, 0), (0, T_PAD - t), (0, 0))).reshape(db * T_PAD, d)
    q8, k8, v8, ga8, g8, vn8, lam_tile = _sample_proj(xs, w_all, ln_g, ln_b, cs, bs8, lam_params, lam_init, t)
    lam = lam_tile[0:1, 0:1]
    ck = cache_k[0].reshape(n_pool, page * H_A, DK)
    cv = cache_v[0].reshape(n_pool, page * H_A, DK)
    a8 = _sample_attn(page_table, lam, q8.reshape(db, T_PAD, W_A), k8.reshape(db, T_PAD * H_A, DK),
                      v8.reshape(db, T_PAD * H_A, DK), ga8.reshape(db, T_PAD, W_A), halfmask, negmask,
                      bias_last, bias_new, sg, ck, cv, lam_init)
    ys = _out_proj(a8.reshape(db * T_PAD, W_A), g8, xs, wo_a, wo_g, pg_, pb_, alpha)
    y_sample = ys.reshape(db, T_PAD, d)[:, :t]
    nks = k8.reshape(db, T_PAD, H_A, DK)[None, :, :t]
    nvs = v8.reshape(db, T_PAD, H_A, DK)[None, :, :t]
    nsv = vn8.reshape(db, T_PAD, H_B, CB)[None, :, :t]

    kf, vf, kb, qt, vt, ga, g = _prompt_proj(x_prompt, w_all, ln_g, ln_b, wt, bsb)
    a = _prompt_attn(lam, qt, kb, vt, bias_prompt, ga, sg, lam_init)
    yp = _out_proj(a.reshape(b * s, W_A), g.reshape(b * s, W_B), x_prompt.reshape(b * s, d),
                   wo_a, wo_g, pg_, pb_, alpha)
    y_prompt = yp.reshape(b, s, d)
    nkp = kf.reshape(1, b, s, H_A, DK)
    nvp = vf.reshape(1, b, s, H_A, DK)
    return (y_prompt, y_sample, nkp, nvp, nks, nvs, nsv)
```

```python
import functools
import math

import numpy as np
import jax
import jax.numpy as jnp
from jax import lax
from jax.experimental import pallas as pl
from jax.experimental.pallas import tpu as pltpu

H_A = 4
HD = 64
DK = 2 * HD
W_A = H_A * DK
H_B = 4
CB = 128
W_B = H_B * CB
CHUNK = 128
N_BUCKETS = 32
MAX_DISTANCE = 128
LN_EPS = 1e-5
SCALE = HD ** -0.5
LOG2E = math.log2(math.e)
NEG = -1e30
MASKED_BUCKET = N_BUCKETS

TM = 512
TM_OUT = 1024
OUT_SUB = 256
TQ = 512
TK = 512
QH = TQ // 2
ONES_ROWS = 16
VA = DK + ONES_ROWS
T_PAD = 8
PAGES_PER_GROUP = 8
RING_GROUPS = 8
VMEM_LIMIT = 56 * 1024 * 1024

_F32 = jnp.float32
_BF16 = jnp.bfloat16
_NT = (((1,), (1,)), ((), ()))


def _bucket_np(dist):
    n = np.maximum(dist, 0)
    max_exact = N_BUCKETS // 2
    nf = np.maximum(n, 1).astype(np.float32)
    large = max_exact + (np.log(nf / max_exact) / math.log(MAX_DISTANCE / max_exact)
                         * (N_BUCKETS - max_exact)).astype(np.int32)
    large = np.minimum(large, N_BUCKETS - 1)
    return np.where(n < max_exact, n, large).astype(np.int32)


def _silu(x):
    return x * jax.nn.sigmoid(x)


def _layer_norm_rows(x, g, b):
    mu = jnp.mean(x, axis=-1, keepdims=True)
    xc = x - mu
    var = jnp.mean(xc * xc, axis=-1, keepdims=True)
    return xc * lax.rsqrt(var + LN_EPS) * g + b


def _rms_norm_rows(x, g):
    return x * lax.rsqrt(jnp.mean(x * x, axis=-1, keepdims=True) + LN_EPS) * g


def _bias_kernel(tab_ref, idx_ref, o_ref):
    h = pl.program_id(0)
    idx = idx_ref[0]
    far = tab_ref[N_BUCKETS - 1, h]
    acc = jnp.full(idx.shape, NEG, _F32)
    for b in range(N_BUCKETS):
        acc = jnp.where(idx == b, (tab_ref[b, h] - far) * LOG2E, acc)
    o_ref[0, 0] = acc


def _bias_tiles(table, idx):
    n, r, c = idx.shape
    return pl.pallas_call(
        _bias_kernel,
        grid=(H_A, n),
        in_specs=[pl.BlockSpec(memory_space=pltpu.SMEM),
                  pl.BlockSpec((1, r, c), lambda h, i: (i, 0, 0))],
        out_specs=pl.BlockSpec((1, 1, r, c), lambda h, i: (h, i, 0, 0)),
        out_shape=jax.ShapeDtypeStruct((H_A, n, r, c), _F32),
        name="bias_tiles",
    )(table, idx)


def _prompt_proj_kernel(x_ref, w_ref, lng_ref, lnb_ref, wt_ref, bs_ref,
                        kf_ref, vf_ref, kb_ref, qt_ref, vt_ref, ga_ref, g_ref):
    xb = x_ref[0].astype(_BF16)
    hm = jnp.dot(xb, w_ref[...], preferred_element_type=_F32)
    k = hm[:, W_A:2 * W_A]
    v = hm[:, 2 * W_A:3 * W_A]
    for h in range(H_A):
        cols = slice(h * DK, (h + 1) * DK)
        kf_ref[0, pl.ds(h, TM, stride=H_A), :] = k[:, cols]
        vf_ref[0, pl.ds(h, TM, stride=H_A), :] = v[:, cols]
    kb_ref[0] = k.astype(_BF16)
    ga_ref[0] = hm[:, 3 * W_A:4 * W_A]
    qt_ref[0] = hm[:, 0:W_A].T.astype(_BF16)
    vt = v.T.astype(_BF16)
    for h in range(H_A):
        vt_ref[0, h, 0:DK, :] = vt[h * DK:(h + 1) * DK]
        vt_ref[0, h, DK:VA, :] = jnp.ones((ONES_ROWS, TM), _BF16)
    o = 4 * W_A
    u = hm[:, o:o + W_B]
    vb = hm[:, o + W_B:o + 2 * W_B]
    gb = hm[:, o + 2 * W_B:o + 3 * W_B]
    vn = _layer_norm_rows(vb, lng_ref[...], lnb_ref[...]).astype(_BF16)
    gate = u * _silu(gb)
    n_chunks = TM // CHUNK
    for h in range(H_B):
        cols = slice(h * CB, (h + 1) * CB)
        vn_h = jnp.concatenate([vn[c * CHUNK:(c + 1) * CHUNK, cols] for c in range(n_chunks)], axis=1)
        s = jnp.dot(wt_ref[h], vn_h, preferred_element_type=_F32)
        for c in range(n_chunks):
            rows = slice(c * CHUNK, (c + 1) * CHUNK)
            g_ref[0, rows, cols] = (gate[rows, cols] * (s[:, c * CB:(c + 1) * CB] + bs_ref[h])).astype(_BF16)


def _prompt_proj(x, w_all, ln_g, ln_b, wt, bsb):
    b, s, d = x.shape
    row = lambda bi, si: (bi, si, 0)
    col = lambda bi, si: (bi, 0, si)
    const2 = lambda bi, si: (0, 0)
    const3 = lambda bi, si: (0, 0, 0)
    f32_heads = jax.ShapeDtypeStruct((b, s * H_A, DK), _F32)
    bf_rows = jax.ShapeDtypeStruct((b, s, W_A), _BF16)
    bf_cols = jax.ShapeDtypeStruct((b, W_A, s), _BF16)
    return pl.pallas_call(
        _prompt_proj_kernel,
        grid=(b, s // TM),
        in_specs=[pl.BlockSpec((1, TM, d), row),
                  pl.BlockSpec(w_all.shape, const2),
                  pl.BlockSpec(ln_g.shape, const2),
                  pl.BlockSpec(ln_b.shape, const2),
                  pl.BlockSpec(wt.shape, const3),
                  pl.BlockSpec(bsb.shape, const3)],
        out_specs=[pl.BlockSpec((1, TM * H_A, DK), row), pl.BlockSpec((1, TM * H_A, DK), row),
                   pl.BlockSpec((1, TM, W_A), row),
                   pl.BlockSpec((1, W_A, TM), col),
                   pl.BlockSpec((1, H_A, VA, TM), lambda bi, si: (bi, 0, 0, si)),
                   pl.BlockSpec((1, TM, W_A), row), pl.BlockSpec((1, TM, W_B), row)],
        out_shape=[f32_heads, f32_heads, bf_rows, bf_cols,
                   jax.ShapeDtypeStruct((b, H_A, VA, s), _BF16),
                   jax.ShapeDtypeStruct((b, s, W_A), _F32), jax.ShapeDtypeStruct((b, s, W_B), _BF16)],
        compiler_params=pltpu.CompilerParams(dimension_semantics=("arbitrary", "arbitrary"),
                                             vmem_limit_bytes=VMEM_LIMIT),
        name="prompt_proj",
    )(x, w_all, ln_g, ln_b, wt, bsb)


def _prompt_attn_kernel(lam_ref, qt_ref, k_ref, vt_ref, bias_ref, ga_ref, sg_ref, a_ref,
                        qz_ref, qzn_ref, sa_ref, sb_ref, sc_ref, mxa_ref, mxb_ref, mxc_ref, m_ref, l_ref, acc_ref,
                        *, lam_init):
    qi = pl.program_id(2)
    n_q = pl.num_programs(2)

    def split_maps(qt, dst_ref):
        row = lax.broadcasted_iota(jnp.int32, qt.shape, 0)
        zero = jnp.zeros_like(qt)
        dst_ref[:, 0:TQ] = jnp.where(row < HD, qt, zero)
        dst_ref[:, TQ:2 * TQ] = jnp.where(row >= HD, qt, zero)

    def begin():
        q_start = pl.multiple_of(qi * TQ, TQ)
        split_maps(qt_ref[0, :, pl.ds(q_start, TQ)], qz_ref)
        m_ref[...] = jnp.full(m_ref.shape, NEG, _F32)
        l_ref[...] = jnp.zeros(l_ref.shape, _F32)
        acc_ref[...] = jnp.zeros(acc_ref.shape, _F32)

    def finish():
        lam = lam_ref[0, 0]
        o_t = acc_ref[0] * (1.0 / l_ref[0]) - lam * (acc_ref[1] * (1.0 / l_ref[1]))
        o = o_t.T
        o = _rms_norm_rows(o, sg_ref[...]) * (1.0 - lam_init)
        a_ref[0] = (o * _silu(ga_ref[0])).astype(_BF16)

    def scores(j, dst_ref, mx_ref, bias_idx=None, q_ref=qz_ref):
        start = pl.multiple_of(j * TK, TK)
        s = jnp.dot(k_ref[0, pl.ds(start, TK), :], q_ref[...], preferred_element_type=_F32)
        if bias_idx == 1:
            bias = bias_ref[0, 1]
            s = jnp.concatenate([s[:, 0:TQ] + bias, s[:, TQ:2 * TQ] + bias], axis=1)
        elif bias_idx == 0:
            md = MAX_DISTANCE
            corner = bias_ref[0, 0, TK - md:TK, 0:md]
            low = s[TK - md:TK]
            low = jnp.concatenate([low[:, 0:md] + corner, low[:, md:TQ],
                                   low[:, TQ:TQ + md] + corner, low[:, TQ + md:2 * TQ]], axis=1)
            s = jnp.concatenate([s[0:TK - md], low], axis=0)
        dst_ref[...] = s
        mx_ref[...] = jnp.max(s, axis=0, keepdims=True)

    def consume(src_ref, mx_ref, j, diagonal=False):
        start = pl.multiple_of(j * TK, TK)
        vblk = vt_ref[0, 0, :, pl.ds(start, TK)]
        for mp in range(2):
            for hf in range(TQ // QH):
                qs = slice(hf * QH, (hf + 1) * QH)
                cols = slice(mp * TQ + hf * QH, mp * TQ + (hf + 1) * QH)
                n_keys = (hf + 1) * QH if diagonal else TK
                m_old = m_ref[mp, :, qs]
                m_new = jnp.maximum(m_old, mx_ref[:, cols])
                alpha = jnp.exp2(m_old - m_new)
                p = jnp.exp2(src_ref[0:n_keys, cols] - m_new).astype(_BF16)
                pv = jnp.dot(vblk[:, 0:n_keys], p, preferred_element_type=_F32)
                acc_ref[mp, :, qs] = acc_ref[mp, :, qs] * alpha + pv[0:DK]
                l_ref[mp, :, qs] = l_ref[mp, :, qs] * alpha + pv[DK:DK + 1]
                m_ref[mp, :, qs] = m_new

    def next_diagonal():
        jn = jnp.minimum(qi + 1, n_q - 1)
        split_maps(qt_ref[0, :, pl.ds(pl.multiple_of(jn * TQ, TQ), TQ)], qzn_ref)
        scores(jn, sc_ref, mxc_ref, bias_idx=1, q_ref=qzn_ref)

    n_far = jnp.maximum(qi - 1, 0)
    n_rest = jnp.maximum(n_far - 1, 0)
    odd_rest = jnp.bitwise_and(n_rest, 1)

    @pl.when(qi == 0)
    def _():
        begin()
        scores(0, sc_ref, mxc_ref, bias_idx=1)
        consume(sc_ref, mxc_ref, 0, diagonal=True)
        finish()

    @pl.when(qi == 1)
    def _():
        begin()
        scores(1, sc_ref, mxc_ref, bias_idx=1)
        scores(0, sb_ref, mxb_ref, bias_idx=0)
        consume(sc_ref, mxc_ref, 1, diagonal=True)
        consume(sb_ref, mxb_ref, 0)
        next_diagonal()
        finish()

    @pl.when(qi >= 2)
    def _():
        begin()
        scores(qi - 1, sb_ref, mxb_ref, bias_idx=0)
        consume(sc_ref, mxc_ref, qi, diagonal=True)
        scores(0, sa_ref, mxa_ref)
        consume(sb_ref, mxb_ref, qi - 1)

    def far_tiles(f0, count):
        for t in range(0, count, 2):
            scores(f0 + t + 1, sb_ref, mxb_ref)
            consume(sa_ref, mxa_ref, f0 + t)
            scores(f0 + t + 2, sa_ref, mxa_ref)
            consume(sb_ref, mxb_ref, f0 + t + 1)

    def far_quad(i, carry):
        far_tiles(4 * i, 4)
        return carry

    n_quads = lax.shift_right_logical(n_rest, 2)
    lax.fori_loop(0, n_quads, far_quad, 0)

    @pl.when(jnp.bitwise_and(n_rest, 2) == 2)
    def _():
        far_tiles(4 * n_quads, 2)

    @pl.when(jnp.logical_and(qi >= 2, odd_rest == 0))
    def _():
        consume(sa_ref, mxa_ref, n_far - 1)
        next_diagonal()
        finish()

    @pl.when(jnp.logical_and(qi >= 2, odd_rest == 1))
    def _():
        scores(n_far - 1, sb_ref, mxb_ref)
        consume(sa_ref, mxa_ref, n_far - 2)
        consume(sb_ref, mxb_ref, n_far - 1)
        next_diagonal()
        finish()


def _prompt_attn(lam, qt, kb, vt, bias, ga, subln_g, lam_init):
    b, s, _ = kb.shape
    return pl.pallas_call(
        functools.partial(_prompt_attn_kernel, lam_init=lam_init),
        grid=(b, H_A, s // TQ),
        in_specs=[pl.BlockSpec(memory_space=pltpu.SMEM),
                  pl.BlockSpec((1, DK, s), lambda bi, h, qi: (bi, h, 0)),
                  pl.BlockSpec((1, s, DK), lambda bi, h, qi: (bi, 0, h)),
                  pl.BlockSpec((1, 1, VA, s), lambda bi, h, qi: (bi, h, 0, 0)),
                  pl.BlockSpec((1, 2, TK, TQ), lambda bi, h, qi: (h, 0, 0, 0)),
                  pl.BlockSpec((1, TQ, DK), lambda bi, h, qi: (bi, qi, h)),
                  pl.BlockSpec(subln_g.shape, lambda bi, h, qi: (0, 0))],
        out_specs=pl.BlockSpec((1, TQ, DK), lambda bi, h, qi: (bi, qi, h)),
        out_shape=jax.ShapeDtypeStruct((b, s, W_A), _BF16),
        scratch_shapes=[pltpu.VMEM((DK, 2 * TQ), _BF16),
                        pltpu.VMEM((DK, 2 * TQ), _BF16),
                        pltpu.VMEM((TK, 2 * TQ), _F32),
                        pltpu.VMEM((TK, 2 * TQ), _F32),
                        pltpu.VMEM((TK, 2 * TQ), _F32),
                        pltpu.VMEM((1, 2 * TQ), _F32),
                        pltpu.VMEM((1, 2 * TQ), _F32),
                        pltpu.VMEM((1, 2 * TQ), _F32),
                        pltpu.VMEM((2, 1, TQ), _F32),
                        pltpu.VMEM((2, 1, TQ), _F32),
                        pltpu.VMEM((2, DK, TQ), _F32)],
        compiler_params=pltpu.CompilerParams(dimension_semantics=("arbitrary", "arbitrary", "arbitrary"),
                                             vmem_limit_bytes=VMEM_LIMIT),
        name="prompt_attn",
    )(lam, qt, kb, vt, bias, ga, subln_g)


def _out_proj_kernel(a_ref, g_ref, x_ref, wa_ref, wg_ref, lng_ref, lnb_ref, y_ref, *, alpha):
    sub = min(OUT_SUB, a_ref.shape[0])
    for i in range(a_ref.shape[0] // sub):
        rows = slice(i * sub, (i + 1) * sub)
        out = (jnp.dot(a_ref[rows, :], wa_ref[...], preferred_element_type=_F32)
               + jnp.dot(g_ref[rows, :], wg_ref[...], preferred_element_type=_F32))
        z = alpha * x_ref[rows, :] + out
        y_ref[rows, :] = _layer_norm_rows(z, lng_ref[...], lnb_ref[...])


def _out_proj(a, g, x, wo_a, wo_g, ln_g, ln_b, alpha):
    n, d = x.shape
    tm = min(n, TM_OUT)
    assert n % tm == 0
    row = lambda i: (i, 0)
    const = lambda i: (0, 0)
    return pl.pallas_call(
        functools.partial(_out_proj_kernel, alpha=alpha),
        grid=(n // tm,),
        in_specs=[pl.BlockSpec((tm, W_A), row), pl.BlockSpec((tm, W_B), row), pl.BlockSpec((tm, d), row),
                  pl.BlockSpec(wo_a.shape, const), pl.BlockSpec(wo_g.shape, const),
                  pl.BlockSpec(ln_g.shape, const), pl.BlockSpec(ln_b.shape, const)],
        out_specs=pl.BlockSpec((tm, d), row),
        out_shape=jax.ShapeDtypeStruct((n, d), _F32),
        compiler_params=pltpu.CompilerParams(dimension_semantics=("arbitrary",),
                                             vmem_limit_bytes=VMEM_LIMIT),
        name="out_proj",
    )(a, g, x, wo_a, wo_g, ln_g, ln_b)


def _sample_proj_kernel(x_ref, w_ref, lng_ref, lnb_ref, cs_ref, bs_ref, lp_ref,
                        q_ref, k_ref, v_ref, ga_ref, g_ref, vn_ref, lam_ref, *, lam_init, t_valid):
    rows = x_ref.shape[0]
    h = jnp.dot(x_ref[...].astype(_BF16), w_ref[...], preferred_element_type=_F32)
    q_ref[...] = h[:, 0:W_A]
    k_ref[...] = h[:, W_A:2 * W_A]
    v_ref[...] = h[:, 2 * W_A:3 * W_A]
    ga_ref[...] = h[:, 3 * W_A:4 * W_A]
    o = 4 * W_A
    u = h[:, o:o + W_B]
    vb = h[:, o + W_B:o + 2 * W_B]
    gb = h[:, o + 2 * W_B:o + 3 * W_B]
    vn = _layer_norm_rows(vb, lng_ref[...], lnb_ref[...])
    vn_ref[...] = vn
    vn3 = vn.reshape(rows // T_PAD, T_PAD, W_B)
    s3 = jnp.broadcast_to(bs_ref[...][None], vn3.shape)
    for j in range(t_valid):
        s3 = s3 + vn3[:, j:j + 1, :] * cs_ref[j][None]
    g_ref[...] = (u * s3.reshape(rows, W_B) * _silu(gb)).astype(_BF16)
    lp = lp_ref[...]
    e1 = jnp.exp(jnp.sum(lp[0:1] * lp[1:2], axis=-1, keepdims=True))
    e2 = jnp.exp(jnp.sum(lp[2:3] * lp[3:4], axis=-1, keepdims=True))
    lam_ref[...] = jnp.broadcast_to(e1 - e2 + lam_init, lam_ref.shape)


def _sample_proj(x, w, ln_g, ln_b, cs, bs8, lam_params, lam_init, t_valid):
    n, d = x.shape
    tm = min(n, 256)
    row = lambda i: (i, 0)
    const2 = lambda i: (0, 0)
    const3 = lambda i: (0, 0, 0)
    f32_rows = jax.ShapeDtypeStruct((n, W_A), _F32)
    return pl.pallas_call(
        functools.partial(_sample_proj_kernel, lam_init=lam_init, t_valid=t_valid),
        grid=(n // tm,),
        in_specs=[pl.BlockSpec((tm, d), row), pl.BlockSpec(w.shape, const2),
                  pl.BlockSpec(ln_g.shape, const2), pl.BlockSpec(ln_b.shape, const2),
                  pl.BlockSpec(cs.shape, const3), pl.BlockSpec(bs8.shape, const2),
                  pl.BlockSpec(lam_params.shape, const2)],
        out_specs=[pl.BlockSpec((tm, W_A), row)] * 4
                  + [pl.BlockSpec((tm, W_B), row), pl.BlockSpec((tm, W_B), row),
                     pl.BlockSpec((8, 128), const2)],
        out_shape=[f32_rows] * 4 + [jax.ShapeDtypeStruct((n, W_B), _BF16),
                                    jax.ShapeDtypeStruct((n, W_B), _F32),
                                    jax.ShapeDtypeStruct((8, 128), _F32)],
        compiler_params=pltpu.CompilerParams(dimension_semantics=("arbitrary",),
                                             vmem_limit_bytes=VMEM_LIMIT),
        name="sample_proj",
    )(x, w, ln_g, ln_b, cs, bs8, lam_params)


def _sample_attn_kernel(pt_ref, lam_ref, q_ref, kn_ref, vn_ref, ga_ref, hm_ref, neg_ref, bl_ref, bn_ref, sg_ref,
                        ck_ref, cv_ref, a_ref, kbuf, vbuf, ksem, vsem, qm_ref, s0_ref, s1_ref, m_ref, l_ref, acc_ref,
                        *, lam_init, n_groups, n_seq):
    gp = PAGES_PER_GROUP
    b = pl.program_id(0)
    n_rows = 2 * H_A * T_PAD
    page_rows = kbuf.shape[1]

    def group_copies(hbm_ref, buf, sem, seq, grp):
        slot = jnp.bitwise_and(grp, RING_GROUPS - 1)
        return [pltpu.make_async_copy(hbm_ref.at[pt_ref[seq, grp * gp + i]], buf.at[slot * gp + i], sem.at[slot])
                for i in range(gp)]

    def start_ahead(hbm_ref, buf, sem, grp):
        ahead = jnp.asarray(grp + RING_GROUPS, jnp.int32)
        wraps = (ahead >= n_groups).astype(jnp.int32)
        seq = b + wraps
        grp2 = ahead - wraps * n_groups

        @pl.when(seq < n_seq)
        def _():
            for i, c in enumerate(group_copies(hbm_ref, buf, sem, seq, grp2)):
                c.start(priority=i % 2)

    def wait_group(hbm_ref, buf, sem, grp):
        for c in group_copies(hbm_ref, buf, sem, b, grp):
            c.wait()

    @pl.when(b == 0)
    def _():
        for g in range(RING_GROUPS):
            for i, c in enumerate(group_copies(ck_ref, kbuf, ksem, 0, g) + group_copies(cv_ref, vbuf, vsem, 0, g)):
                c.start(priority=i % 2)

    q8 = q_ref[0]
    per_head = [q8[:, h * DK:(h + 1) * DK] for h in range(H_A)]
    qm_ref[...] = jnp.concatenate(per_head + per_head, axis=0) * hm_ref[...]
    m_ref[...] = jnp.full(m_ref.shape, NEG, _F32)
    l_ref[...] = jnp.zeros(l_ref.shape, _F32)
    acc_ref[...] = jnp.zeros(acc_ref.shape, _F32)

    def scores(grp, dst_ref, with_last_bias):
        slot = jnp.bitwise_and(grp, RING_GROUPS - 1)
        qm = qm_ref[...]
        for i in range(gp):
            s = lax.dot_general(qm, kbuf[slot * gp + i], _NT, preferred_element_type=_F32) + neg_ref[...]
            if with_last_bias and i == gp - 1:
                s = s + bl_ref[...]
            dst_ref[:, i * page_rows:(i + 1) * page_rows] = s

    def consume(grp, src_ref):
        slot = jnp.bitwise_and(grp, RING_GROUPS - 1)
        smax = src_ref[:, 0:page_rows]
        for i in range(1, gp):
            smax = jnp.maximum(smax, src_ref[:, i * page_rows:(i + 1) * page_rows])
        m_old = m_ref[...]
        m_new = jnp.maximum(m_old, jnp.max(smax, axis=-1, keepdims=True))
        alpha = jnp.exp2(m_old - m_new)
        l = alpha * l_ref[...]
        acc = alpha * acc_ref[...]
        for i in range(gp):
            p = jnp.exp2(src_ref[:, i * page_rows:(i + 1) * page_rows] - m_new)
            l = l + p
            acc = acc + jnp.dot(p, vbuf[slot * gp + i], preferred_element_type=_F32)
        l_ref[...] = l
        acc_ref[...] = acc
        m_ref[...] = m_new

    def block(grp, dst_ref, src_ref, with_last_bias=False):
        wait_group(ck_ref, kbuf, ksem, grp)
        wait_group(cv_ref, vbuf, vsem, grp - 1)
        scores(grp, dst_ref, with_last_bias)
        consume(grp - 1, src_ref)
        start_ahead(ck_ref, kbuf, ksem, grp)
        start_ahead(cv_ref, vbuf, vsem, grp - 1)

    wait_group(ck_ref, kbuf, ksem, 0)
    scores(0, s0_ref, False)
    start_ahead(ck_ref, kbuf, ksem, 0)

    def pair(i, carry):
        g = 2 * i + 1
        block(g, s1_ref, s0_ref)
        block(g + 1, s0_ref, s1_ref)
        return carry

    lax.fori_loop(0, (n_groups - 2) // 2, pair, 0)
    block(n_groups - 1, s1_ref, s0_ref, with_last_bias=True)
    wait_group(cv_ref, vbuf, vsem, n_groups - 1)
    consume(n_groups - 1, s1_ref)
    start_ahead(cv_ref, vbuf, vsem, n_groups - 1)

    sn = lax.dot_general(qm_ref[...], kn_ref[0], _NT, preferred_element_type=_F32) + bn_ref[...]
    m_o = m_ref[...]
    m_f = jnp.maximum(m_o, jnp.max(sn, axis=-1, keepdims=True))
    al = jnp.exp2(m_o - m_f)
    pn = jnp.exp2(sn - m_f)
    l_tot = al * jnp.sum(l_ref[...], axis=-1, keepdims=True) + jnp.sum(pn, axis=-1, keepdims=True)
    acc = al * acc_ref[...] + jnp.dot(pn, vn_ref[0], preferred_element_type=_F32)
    o_full = acc * (1.0 / l_tot)
    half = n_rows // 2
    od = o_full[0:half] - lam_ref[0, 0] * o_full[half:n_rows]
    ga = ga_ref[0]
    for h in range(H_A):
        cols = slice(h * DK, (h + 1) * DK)
        oh = _rms_norm_rows(od[h * T_PAD:(h + 1) * T_PAD], sg_ref[...]) * (1.0 - lam_init)
        a_ref[0, :, cols] = (oh * _silu(ga[:, cols])).astype(_BF16)


def _sample_attn(page_table, lam, q8, kn, vn, ga8, halfmask, negmask, bias_last, bias_new, subln_g, ck, cv,
                 lam_init):
    db, n_pages = page_table.shape
    gp = PAGES_PER_GROUP
    n_groups = n_pages // gp
    assert n_pages % gp == 0 and n_groups % RING_GROUPS == 0 and n_groups % 2 == 0
    page_rows = ck.shape[1]
    n_rows = 2 * H_A * T_PAD
    seq = lambda bi, pt: (bi, 0, 0)
    const = lambda bi, pt: (0, 0)
    grid_spec = pltpu.PrefetchScalarGridSpec(
        num_scalar_prefetch=1,
        grid=(db,),
        in_specs=[pl.BlockSpec(memory_space=pltpu.SMEM),
                  pl.BlockSpec((1, T_PAD, W_A), seq),
                  pl.BlockSpec((1, T_PAD * H_A, DK), seq), pl.BlockSpec((1, T_PAD * H_A, DK), seq),
                  pl.BlockSpec((1, T_PAD, W_A), seq),
                  pl.BlockSpec(halfmask.shape, const), pl.BlockSpec(negmask.shape, const),
                  pl.BlockSpec(bias_last.shape, const), pl.BlockSpec(bias_new.shape, const),
                  pl.BlockSpec(subln_g.shape, const),
                  pl.BlockSpec(memory_space=pl.ANY), pl.BlockSpec(memory_space=pl.ANY)],
        out_specs=pl.BlockSpec((1, T_PAD, W_A), seq),
        scratch_shapes=[pltpu.VMEM((RING_GROUPS * gp, page_rows, DK), _F32),
                        pltpu.VMEM((RING_GROUPS * gp, page_rows, DK), _F32),
                        pltpu.SemaphoreType.DMA((RING_GROUPS,)),
                        pltpu.SemaphoreType.DMA((RING_GROUPS,)),
                        pltpu.VMEM((n_rows, DK), _F32),
                        pltpu.VMEM((n_rows, gp * page_rows), _F32),
                        pltpu.VMEM((n_rows, gp * page_rows), _F32),
                        pltpu.VMEM((n_rows, 1), _F32),
                        pltpu.VMEM((n_rows, page_rows), _F32),
                        pltpu.VMEM((n_rows, DK), _F32)],
    )
    return pl.pallas_call(
        functools.partial(_sample_attn_kernel, lam_init=lam_init, n_groups=n_groups, n_seq=db),
        grid_spec=grid_spec,
        out_shape=jax.ShapeDtypeStruct((db, T_PAD, W_A), _BF16),
        compiler_params=pltpu.CompilerParams(dimension_semantics=("arbitrary",),
                                             vmem_limit_bytes=VMEM_LIMIT),
        name="sample_attn",
    )(page_table, lam, q8, kn, vn, ga8, halfmask, negmask, bias_last, bias_new, subln_g, ck, cv)


def kernel(x_prompt, x_sample, cache_k, cache_v, page_table, w_in, w_out, lambda_q1, lambda_k1, lambda_q2,
           lambda_k2, subln_g, rel_bias, sgu_ln_g, sgu_ln_b, sgu_w, sgu_b, post_ln_g, post_ln_b):
    depth = w_in.shape[0]
    assert depth == 1, "single trunk layer"
    b, s, d = x_prompt.shape
    db, t, _ = x_sample.shape
    n_pool, page = cache_k.shape[1], cache_k.shape[2]
    assert s % TQ == 0 and s % TM == 0 and TQ == TK and TQ >= MAX_DISTANCE
    assert t <= T_PAD and page >= MAX_DISTANCE
    lam_init = 0.8 - 0.6 * math.exp(-0.3 * 0)
    alpha = (2.0 * depth) ** 0.25

    col_scale = jnp.where(jnp.arange(w_in.shape[2]) < W_A, SCALE * LOG2E, 1.0).astype(_F32)
    w_all = (w_in[0] * col_scale[None, :]).astype(_BF16)
    wo_a = w_out[0, 0:W_A].astype(_BF16)
    wo_g = w_out[0, W_A:].astype(_BF16)
    ln_g = sgu_ln_g[0][None]
    ln_b = sgu_ln_b[0][None]
    pg_ = post_ln_g[0][None]
    pb_ = post_ln_b[0][None]
    sg = subln_g[0][None]
    tril = jnp.tril(jnp.ones((CHUNK, CHUNK), _F32))
    wt = (sgu_w[0] * tril).astype(_BF16)
    bsb = jnp.broadcast_to(sgu_b[0][:, :, None], (H_B, CHUNK, CB))
    wsmall = sgu_w[0][:, :T_PAD, :T_PAD] * tril[:T_PAD, :T_PAD]
    valid_t = (jnp.arange(T_PAD) < t).astype(_F32)
    cs = jnp.transpose(wsmall, (2, 1, 0)) * valid_t[None, :, None]
    cs = jnp.repeat(cs, CB, axis=2)
    bs8 = jnp.repeat(jnp.transpose(sgu_b[0][:, :T_PAD]) * valid_t[:, None], CB, axis=1)
    lam_params = jnp.concatenate([lambda_q1, lambda_k1, lambda_q2, lambda_k2], axis=0)

    kk = np.arange(TK)[:, None]
    qq = np.arange(TQ)[None, :]
    idx_sub = _bucket_np(TQ + qq - kk)
    idx_diag = np.where(qq >= kk, _bucket_np(qq - kk), MASKED_BUCKET)
    bias_prompt = _bias_tiles(rel_bias, jnp.asarray(np.stack([idx_sub, idx_diag]).astype(np.int32)))
    tt = np.arange(T_PAD)[:, None]
    ii = np.arange(page * H_A)[None, :] // H_A
    idx_last = _bucket_np(np.where(tt < t, page + tt - ii, MAX_DISTANCE))
    idx_new = np.where((ii <= tt) & (tt < t), _bucket_np(tt - ii), MASKED_BUCKET)
    bias_sample = _bias_tiles(rel_bias, jnp.asarray(np.stack([idx_last, idx_new]).astype(np.int32)))
    r = np.arange(2 * H_A * T_PAD)
    r_head = (r // T_PAD) % H_A
    r_map = r // (H_A * T_PAD)
    lane_head = np.arange(page * H_A) % H_A
    own_head = r_head[:, None] == lane_head[None, :]
    negmask = jnp.asarray(np.where(own_head, 0.0, NEG).astype(np.float32))
    halfmask = jnp.asarray((np.arange(DK)[None, :] // HD == r_map[:, None]).astype(np.float32))
    bias_last = jnp.tile(bias_sample[:, 0].reshape(H_A * T_PAD, page * H_A), (2, 1))
    bias_new = (jnp.tile(bias_sample[:, 1, :, :T_PAD * H_A].reshape(H_A * T_PAD, T_PAD * H_A), (2, 1))
                + negmask[:, :T_PAD * H_A])

    xs = jnp.pad(x_sample, ((0, 0), (0, T_PAD - t), (0, 0))).reshape(db * T_PAD, d)
    q8, k8, v8, ga8, g8, vn8, lam_tile = _sample_proj(xs, w_all, ln_g, ln_b, cs, bs8, lam_params, lam_init, t)
    lam = lam_tile[0:1, 0:1]
    ck = cache_k[0].reshape(n_pool, page * H_A, DK)
    cv = cache_v[0].reshape(n_pool, page * H_A, DK)
    a8 = _sample_attn(page_table, lam, q8.reshape(db, T_PAD, W_A), k8.reshape(db, T_PAD * H_A, DK),
                      v8.reshape(db, T_PAD * H_A, DK), ga8.reshape(db, T_PAD, W_A), halfmask, negmask,
                      bias_last, bias_new, sg, ck, cv, lam_init)
    ys = _out_proj(a8.reshape(db * T_PAD, W_A), g8, xs, wo_a, wo_g, pg_, pb_, alpha)
    y_sample = ys.reshape(db, T_PAD, d)[:, :t]
    nks = k8.reshape(db, T_PAD, H_A, DK)[None, :, :t]
    nvs = v8.reshape(db, T_PAD, H_A, DK)[None, :, :t]
    nsv = vn8.reshape(db, T_PAD, H_B, CB)[None, :, :t]

    kf, vf, kb, qt, vt, ga, g = _prompt_proj(x_prompt, w_all, ln_g, ln_b, wt, bsb)
    a = _prompt_attn(lam, qt, kb, vt, bias_prompt, ga, sg, lam_init)
    yp = _out_proj(a.reshape(b * s, W_A), g.reshape(b * s, W_B), x_prompt.reshape(b * s, d),
                   wo_a, wo_g, pg_, pb_, alpha)
    y_prompt = yp.reshape(b, s, d)
    nkp = kf.reshape(1, b, s, H_A, DK)
    nvp = vf.reshape(1, b, s, H_A, DK)
    return (y_prompt, y_sample, nkp, nvp, nks, nvs, nsv)
```

```python
import functools
import math

import numpy as np
import jax
import jax.numpy as jnp
from jax import lax
from jax.experimental import pallas as pl
from jax.experimental.pallas import tpu as pltpu

H_A = 4
HD = 64
DK = 2 * HD
W_A = H_A * DK
H_B = 4
CB = 128
W_B = H_B * CB
CHUNK = 128
N_BUCKETS = 32
MAX_DISTANCE = 128
LN_EPS = 1e-5
SCALE = HD ** -0.5
LOG2E = math.log2(math.e)
NEG = -1e30
MASKED_BUCKET = N_BUCKETS

TM = 512
TM_OUT = 1024
OUT_SUB = 256
TQ = 512
TK = 512
QH = TQ // 2
ONES_ROWS = 16
VA = DK + ONES_ROWS
T_PAD = 8
PAGES_PER_GROUP = 8
RING_GROUPS = 8
VMEM_LIMIT = 56 * 1024 * 1024

_F32 = jnp.float32
_BF16 = jnp.bfloat16
_NT = (((1,), (1,)), ((), ()))


def _bucket_np(dist):
    n = np.maximum(dist, 0)
    max_exact = N_BUCKETS // 2
    nf = np.maximum(n, 1).astype(np.float32)
    large = max_exact + (np.log(nf / max_exact) / math.log(MAX_DISTANCE / max_exact)
                         * (N_BUCKETS - max_exact)).astype(np.int32)
    large = np.minimum(large, N_BUCKETS - 1)
    return np.where(n < max_exact, n, large).astype(np.int32)


def _silu(x):
    return x * jax.nn.sigmoid(x)


def _layer_norm_rows(x, g, b):
    mu = jnp.mean(x, axis=-1, keepdims=True)
    xc = x - mu
    var = jnp.mean(xc * xc, axis=-1, keepdims=True)
    return xc * lax.rsqrt(var + LN_EPS) * g + b


def _rms_norm_rows(x, g):
    return x * lax.rsqrt(jnp.mean(x * x, axis=-1, keepdims=True) + LN_EPS) * g


def _bias_kernel(tab_ref, idx_ref, o_ref):
    h = pl.program_id(0)
    idx = idx_ref[0]
    far = tab_ref[N_BUCKETS - 1, h]
    acc = jnp.full(idx.shape, NEG, _F32)
    for b in range(N_BUCKETS):
        acc = jnp.where(idx == b, (tab_ref[b, h] - far) * LOG2E, acc)
    o_ref[0, 0] = acc


def _bias_tiles(table, idx):
    n, r, c = idx.shape
    return pl.pallas_call(
        _bias_kernel,
        grid=(H_A, n),
        in_specs=[pl.BlockSpec(memory_space=pltpu.SMEM),
                  pl.BlockSpec((1, r, c), lambda h, i: (i, 0, 0))],
        out_specs=pl.BlockSpec((1, 1, r, c), lambda h, i: (h, i, 0, 0)),
        out_shape=jax.ShapeDtypeStruct((H_A, n, r, c), _F32),
        name="bias_tiles",
    )(table, idx)


def _prompt_proj_kernel(x_ref, w_ref, lng_ref, lnb_ref, wt_ref, bs_ref,
                        kf_ref, vf_ref, kb_ref, qt_ref, vt_ref, ga_ref, g_ref):
    xb = x_ref[0].astype(_BF16)
    hm = jnp.dot(xb, w_ref[...], preferred_element_type=_F32)
    k = hm[:, W_A:2 * W_A]
    v = hm[:, 2 * W_A:3 * W_A]
    for h in range(H_A):
        cols = slice(h * DK, (h + 1) * DK)
        kf_ref[0, pl.ds(h, TM, stride=H_A), :] = k[:, cols]
        vf_ref[0, pl.ds(h, TM, stride=H_A), :] = v[:, cols]
    kb_ref[0] = k.astype(_BF16)
    ga_ref[0] = hm[:, 3 * W_A:4 * W_A]
    qt_ref[0] = hm[:, 0:W_A].T.astype(_BF16)
    vt = v.T.astype(_BF16)
    for h in range(H_A):
        vt_ref[0, h, 0:DK, :] = vt[h * DK:(h + 1) * DK]
        vt_ref[0, h, DK:VA, :] = jnp.ones((ONES_ROWS, TM), _BF16)
    o = 4 * W_A
    u = hm[:, o:o + W_B]
    vb = hm[:, o + W_B:o + 2 * W_B]
    gb = hm[:, o + 2 * W_B:o + 3 * W_B]
    vn = _layer_norm_rows(vb, lng_ref[...], lnb_ref[...]).astype(_BF16)
    gate = u * _silu(gb)
    n_chunks = TM // CHUNK
    for h in range(H_B):
        cols = slice(h * CB, (h + 1) * CB)
        vn_h = jnp.concatenate([vn[c * CHUNK:(c + 1) * CHUNK, cols] for c in range(n_chunks)], axis=1)
        s = jnp.dot(wt_ref[h], vn_h, preferred_element_type=_F32)
        for c in range(n_chunks):
            rows = slice(c * CHUNK, (c + 1) * CHUNK)
            g_ref[0, rows, cols] = (gate[rows, cols] * (s[:, c * CB:(c + 1) * CB] + bs_ref[h])).astype(_BF16)


def _prompt_proj(x, w_all, ln_g, ln_b, wt, bsb):
    b, s, d = x.shape
    row = lambda bi, si: (bi, si, 0)
    col = lambda bi, si: (bi, 0, si)
    const2 = lambda bi, si: (0, 0)
    const3 = lambda bi, si: (0, 0, 0)
    f32_heads = jax.ShapeDtypeStruct((b, s * H_A, DK), _F32)
    bf_rows = jax.ShapeDtypeStruct((b, s, W_A), _BF16)
    bf_cols = jax.ShapeDtypeStruct((b, W_A, s), _BF16)
    return pl.pallas_call(
        _prompt_proj_kernel,
        grid=(b, s // TM),
        in_specs=[pl.BlockSpec((1, TM, d), row),
                  pl.BlockSpec(w_all.shape, const2),
                  pl.BlockSpec(ln_g.shape, const2),
                  pl.BlockSpec(ln_b.shape, const2),
                  pl.BlockSpec(wt.shape, const3),
                  pl.BlockSpec(bsb.shape, const3)],
        out_specs=[pl.BlockSpec((1, TM * H_A, DK), row), pl.BlockSpec((1, TM * H_A, DK), row),
                   pl.BlockSpec((1, TM, W_A), row),
                   pl.BlockSpec((1, W_A, TM), col),
                   pl.BlockSpec((1, H_A, VA, TM), lambda bi, si: (bi, 0, 0, si)),
                   pl.BlockSpec((1, TM, W_A), row), pl.BlockSpec((1, TM, W_B), row)],
        out_shape=[f32_heads, f32_heads, bf_rows, bf_cols,
                   jax.ShapeDtypeStruct((b, H_A, VA, s), _BF16),
                   jax.ShapeDtypeStruct((b, s, W_A), _F32), jax.ShapeDtypeStruct((b, s, W_B), _BF16)],
        compiler_params=pltpu.CompilerParams(dimension_semantics=("arbitrary", "arbitrary"),
                                             vmem_limit_bytes=VMEM_LIMIT),
        name="prompt_proj",
    )(x, w_all, ln_g, ln_b, wt, bsb)


def _prompt_attn_kernel(lam_ref, qt_ref, k_ref, vt_ref, bias_ref, ga_ref, sg_ref, a_ref,
                        qz_ref, qzn_ref, sa_ref, sb_ref, sc_ref, mxa_ref, mxb_ref, mxc_ref, m_ref, l_ref, acc_ref,
                        *, lam_init):
    qi = pl.program_id(2)
    n_q = pl.num_programs(2)

    def split_maps(qt, dst_ref):
        row = lax.broadcasted_iota(jnp.int32, qt.shape, 0)
        zero = jnp.zeros_like(qt)
        dst_ref[:, 0:TQ] = jnp.where(row < HD, qt, zero)
        dst_ref[:, TQ:2 * TQ] = jnp.where(row >= HD, qt, zero)

    def begin():
        q_start = pl.multiple_of(qi * TQ, TQ)
        split_maps(qt_ref[0, :, pl.ds(q_start, TQ)], qz_ref)
        m_ref[...] = jnp.full(m_ref.shape, NEG, _F32)
        l_ref[...] = jnp.zeros(l_ref.shape, _F32)
        acc_ref[...] = jnp.zeros(acc_ref.shape, _F32)

    def finish():
        lam = lam_ref[0, 0]
        o_t = acc_ref[0] * (1.0 / l_ref[0]) - lam * (acc_ref[1] * (1.0 / l_ref[1]))
        o = o_t.T
        o = _rms_norm_rows(o, sg_ref[...]) * (1.0 - lam_init)
        a_ref[0] = (o * _silu(ga_ref[0])).astype(_BF16)

    def scores(j, dst_ref, mx_ref, bias_idx=None, q_ref=qz_ref):
        start = pl.multiple_of(j * TK, TK)
        s = jnp.dot(k_ref[0, pl.ds(start, TK), :], q_ref[...], preferred_element_type=_F32)
        if bias_idx == 1:
            bias = bias_ref[0, 1]
            s = jnp.concatenate([s[:, 0:TQ] + bias, s[:, TQ:2 * TQ] + bias], axis=1)
        elif bias_idx == 0:
            md = MAX_DISTANCE
            corner = bias_ref[0, 0, TK - md:TK, 0:md]
            low = s[TK - md:TK]
            low = jnp.concatenate([low[:, 0:md] + corner, low[:, md:TQ],
                                   low[:, TQ:TQ + md] + corner, low[:, TQ + md:2 * TQ]], axis=1)
            s = jnp.concatenate([s[0:TK - md], low], axis=0)
        dst_ref[...] = s
        mx_ref[...] = jnp.max(s, axis=0, keepdims=True)

    def consume(src_ref, mx_ref, j, diagonal=False):
        start = pl.multiple_of(j * TK, TK)
        vblk = vt_ref[0, 0, :, pl.ds(start, TK)]
        for mp in range(2):
            for hf in range(TQ // QH):
                qs = slice(hf * QH, (hf + 1) * QH)
                cols = slice(mp * TQ + hf * QH, mp * TQ + (hf + 1) * QH)
                n_keys = (hf + 1) * QH if diagonal else TK
                m_old = m_ref[mp, :, qs]
                m_new = jnp.maximum(m_old, mx_ref[:, cols])
                alpha = jnp.exp2(m_old - m_new)
                p = jnp.exp2(src_ref[0:n_keys, cols] - m_new).astype(_BF16)
                pv = jnp.dot(vblk[:, 0:n_keys], p, preferred_element_type=_F32)
                acc_ref[mp, :, qs] = acc_ref[mp, :, qs] * alpha + pv[0:DK]
                l_ref[mp, :, qs] = l_ref[mp, :, qs] * alpha + pv[DK:DK + 1]
                m_ref[mp, :, qs] = m_new

    def next_diagonal():
        jn = jnp.minimum(qi + 1, n_q - 1)
        split_maps(qt_ref[0, :, pl.ds(pl.multiple_of(jn * TQ, TQ), TQ)], qzn_ref)
        scores(jn, sc_ref, mxc_ref, bias_idx=1, q_ref=qzn_ref)

    n_far = jnp.maximum(qi - 1, 0)
    n_rest = jnp.maximum(n_far - 1, 0)
    odd_rest = jnp.bitwise_and(n_rest, 1)

    @pl.when(qi == 0)
    def _():
        begin()
        scores(0, sc_ref, mxc_ref, bias_idx=1)
        consume(sc_ref, mxc_ref, 0, diagonal=True)
        finish()

    @pl.when(qi == 1)
    def _():
        begin()
        scores(1, sc_ref, mxc_ref, bias_idx=1)
        scores(0, sb_ref, mxb_ref, bias_idx=0)
        consume(sc_ref, mxc_ref, 1, diagonal=True)
        consume(sb_ref, mxb_ref, 0)
        next_diagonal()
        finish()

    @pl.when(qi >= 2)
    def _():
        begin()
        scores(qi - 1, sb_ref, mxb_ref, bias_idx=0)
        consume(sc_ref, mxc_ref, qi, diagonal=True)
        scores(0, sa_ref, mxa_ref)
        consume(sb_ref, mxb_ref, qi - 1)

    def far_tiles(f0, count):
        for t in range(0, count, 2):
            scores(f0 + t + 1, sb_ref, mxb_ref)
            consume(sa_ref, mxa_ref, f0 + t)
            scores(f0 + t + 2, sa_ref, mxa_ref)
            consume(sb_ref, mxb_ref, f0 + t + 1)

    def far_oct(i, carry):
        far_tiles(8 * i, 8)
        return carry

    n_octs = lax.shift_right_logical(n_rest, 3)
    lax.fori_loop(0, n_octs, far_oct, 0)

    @pl.when(jnp.bitwise_and(n_rest, 4) == 4)
    def _():
        far_tiles(8 * n_octs, 4)

    @pl.when(jnp.bitwise_and(n_rest, 2) == 2)
    def _():
        far_tiles(8 * n_octs + jnp.bitwise_and(n_rest, 4), 2)

    @pl.when(jnp.logical_and(qi >= 2, odd_rest == 0))
    def _():
        consume(sa_ref, mxa_ref, n_far - 1)
        next_diagonal()
        finish()

    @pl.when(jnp.logical_and(qi >= 2, odd_rest == 1))
    def _():
        scores(n_far - 1, sb_ref, mxb_ref)
        consume(sa_ref, mxa_ref, n_far - 2)
        consume(sb_ref, mxb_ref, n_far - 1)
        next_diagonal()
        finish()


def _prompt_attn(lam, qt, kb, vt, bias, ga, subln_g, lam_init):
    b, s, _ = kb.shape
    return pl.pallas_call(
        functools.partial(_prompt_attn_kernel, lam_init=lam_init),
        grid=(b, H_A, s // TQ),
        in_specs=[pl.BlockSpec(memory_space=pltpu.SMEM),
                  pl.BlockSpec((1, DK, s), lambda bi, h, qi: (bi, h, 0)),
                  pl.BlockSpec((1, s, DK), lambda bi, h, qi: (bi, 0, h)),
                  pl.BlockSpec((1, 1, VA, s), lambda bi, h, qi: (bi, h, 0, 0)),
                  pl.BlockSpec((1, 2, TK, TQ), lambda bi, h, qi: (h, 0, 0, 0)),
                  pl.BlockSpec((1, TQ, DK), lambda bi, h, qi: (bi, qi, h)),
                  pl.BlockSpec(subln_g.shape, lambda bi, h, qi: (0, 0))],
        out_specs=pl.BlockSpec((1, TQ, DK), lambda bi, h, qi: (bi, qi, h)),
        out_shape=jax.ShapeDtypeStruct((b, s, W_A), _BF16),
        scratch_shapes=[pltpu.VMEM((DK, 2 * TQ), _BF16),
                        pltpu.VMEM((DK, 2 * TQ), _BF16),
                        pltpu.VMEM((TK, 2 * TQ), _F32),
                        pltpu.VMEM((TK, 2 * TQ), _F32),
                        pltpu.VMEM((TK, 2 * TQ), _F32),
                        pltpu.VMEM((1, 2 * TQ), _F32),
                        pltpu.VMEM((1, 2 * TQ), _F32),
                        pltpu.VMEM((1, 2 * TQ), _F32),
                        pltpu.VMEM((2, 1, TQ), _F32),
                        pltpu.VMEM((2, 1, TQ), _F32),
                        pltpu.VMEM((2, DK, TQ), _F32)],
        compiler_params=pltpu.CompilerParams(dimension_semantics=("arbitrary", "arbitrary", "arbitrary"),
                                             vmem_limit_bytes=VMEM_LIMIT),
        name="prompt_attn",
    )(lam, qt, kb, vt, bias, ga, subln_g)


def _out_proj_kernel(a_ref, g_ref, x_ref, wa_ref, wg_ref, lng_ref, lnb_ref, y_ref, *, alpha):
    sub = min(OUT_SUB, a_ref.shape[0])
    for i in range(a_ref.shape[0] // sub):
        rows = slice(i * sub, (i + 1) * sub)
        out = (jnp.dot(a_ref[rows, :], wa_ref[...], preferred_element_type=_F32)
               + jnp.dot(g_ref[rows, :], wg_ref[...], preferred_element_type=_F32))
        z = alpha * x_ref[rows, :] + out
        y_ref[rows, :] = _layer_norm_rows(z, lng_ref[...], lnb_ref[...])


def _out_proj(a, g, x, wo_a, wo_g, ln_g, ln_b, alpha):
    n, d = x.shape
    tm = min(n, TM_OUT)
    assert n % tm == 0
    row = lambda i: (i, 0)
    const = lambda i: (0, 0)
    return pl.pallas_call(
        functools.partial(_out_proj_kernel, alpha=alpha),
        grid=(n // tm,),
        in_specs=[pl.BlockSpec((tm, W_A), row), pl.BlockSpec((tm, W_B), row), pl.BlockSpec((tm, d), row),
                  pl.BlockSpec(wo_a.shape, const), pl.BlockSpec(wo_g.shape, const),
                  pl.BlockSpec(ln_g.shape, const), pl.BlockSpec(ln_b.shape, const)],
        out_specs=pl.BlockSpec((tm, d), row),
        out_shape=jax.ShapeDtypeStruct((n, d), _F32),
        compiler_params=pltpu.CompilerParams(dimension_semantics=("arbitrary",),
                                             vmem_limit_bytes=VMEM_LIMIT),
        name="out_proj",
    )(a, g, x, wo_a, wo_g, ln_g, ln_b)


def _sample_proj_kernel(x_ref, w_ref, lng_ref, lnb_ref, cs_ref, bs_ref, lp_ref,
                        q_ref, k_ref, v_ref, ga_ref, g_ref, vn_ref, lam_ref, *, lam_init, t_valid):
    rows = x_ref.shape[0]
    h = jnp.dot(x_ref[...].astype(_BF16), w_ref[...], preferred_element_type=_F32)
    q_ref[...] = h[:, 0:W_A]
    k_ref[...] = h[:, W_A:2 * W_A]
    v_ref[...] = h[:, 2 * W_A:3 * W_A]
    ga_ref[...] = h[:, 3 * W_A:4 * W_A]
    o = 4 * W_A
    u = h[:, o:o + W_B]
    vb = h[:, o + W_B:o + 2 * W_B]
    gb = h[:, o + 2 * W_B:o + 3 * W_B]
    vn = _layer_norm_rows(vb, lng_ref[...], lnb_ref[...])
    vn_ref[...] = vn
    vn3 = vn.reshape(rows // T_PAD, T_PAD, W_B)
    s3 = jnp.broadcast_to(bs_ref[...][None], vn3.shape)
    for j in range(t_valid):
        s3 = s3 + vn3[:, j:j + 1, :] * cs_ref[j][None]
    g_ref[...] = (u * s3.reshape(rows, W_B) * _silu(gb)).astype(_BF16)
    lp = lp_ref[...]
    e1 = jnp.exp(jnp.sum(lp[0:1] * lp[1:2], axis=-1, keepdims=True))
    e2 = jnp.exp(jnp.sum(lp[2:3] * lp[3:4], axis=-1, keepdims=True))
    lam_ref[...] = jnp.broadcast_to(e1 - e2 + lam_init, lam_ref.shape)


def _sample_proj(x, w, ln_g, ln_b, cs, bs8, lam_params, lam_init, t_valid):
    n, d = x.shape
    tm = min(n, 256)
    row = lambda i: (i, 0)
    const2 = lambda i: (0, 0)
    const3 = lambda i: (0, 0, 0)
    f32_rows = jax.ShapeDtypeStruct((n, W_A), _F32)
    return pl.pallas_call(
        functools.partial(_sample_proj_kernel, lam_init=lam_init, t_valid=t_valid),
        grid=(n // tm,),
        in_specs=[pl.BlockSpec((tm, d), row), pl.BlockSpec(w.shape, const2),
                  pl.BlockSpec(ln_g.shape, const2), pl.BlockSpec(ln_b.shape, const2),
                  pl.BlockSpec(cs.shape, const3), pl.BlockSpec(bs8.shape, const2),
                  pl.BlockSpec(lam_params.shape, const2)],
        out_specs=[pl.BlockSpec((tm, W_A), row)] * 4
                  + [pl.BlockSpec((tm, W_B), row), pl.BlockSpec((tm, W_B), row),
                     pl.BlockSpec((8, 128), const2)],
        out_shape=[f32_rows] * 4 + [jax.ShapeDtypeStruct((n, W_B), _BF16),
                                    jax.ShapeDtypeStruct((n, W_B), _F32),
                                    jax.ShapeDtypeStruct((8, 128), _F32)],
        compiler_params=pltpu.CompilerParams(dimension_semantics=("arbitrary",),
                                             vmem_limit_bytes=VMEM_LIMIT),
        name="sample_proj",
    )(x, w, ln_g, ln_b, cs, bs8, lam_params)


def _sample_attn_kernel(pt_ref, lam_ref, q_ref, kn_ref, vn_ref, ga_ref, hm_ref, neg_ref, bl_ref, bn_ref, sg_ref,
                        ck_ref, cv_ref, a_ref, kbuf, vbuf, ksem, vsem, qm_ref, s0_ref, s1_ref, m_ref, l_ref, acc_ref,
                        *, lam_init, n_groups, n_seq):
    gp = PAGES_PER_GROUP
    b = pl.program_id(0)
    n_rows = 2 * H_A * T_PAD
    page_rows = kbuf.shape[1]

    def group_copies(hbm_ref, buf, sem, seq, grp):
        slot = jnp.bitwise_and(grp, RING_GROUPS - 1)
        return [pltpu.make_async_copy(hbm_ref.at[pt_ref[seq, grp * gp + i]], buf.at[slot * gp + i], sem.at[slot])
                for i in range(gp)]

    def start_ahead(hbm_ref, buf, sem, grp):
        ahead = jnp.asarray(grp + RING_GROUPS, jnp.int32)
        wraps = (ahead >= n_groups).astype(jnp.int32)
        seq = b + wraps
        grp2 = ahead - wraps * n_groups

        @pl.when(seq < n_seq)
        def _():
            for i, c in enumerate(group_copies(hbm_ref, buf, sem, seq, grp2)):
                c.start(priority=i % 2)

    def wait_group(hbm_ref, buf, sem, grp):
        for c in group_copies(hbm_ref, buf, sem, b, grp):
            c.wait()

    @pl.when(b == 0)
    def _():
        for g in range(RING_GROUPS):
            for i, c in enumerate(group_copies(ck_ref, kbuf, ksem, 0, g) + group_copies(cv_ref, vbuf, vsem, 0, g)):
                c.start(priority=i % 2)

    q8 = q_ref[0]
    per_head = [q8[:, h * DK:(h + 1) * DK] for h in range(H_A)]
    qm_ref[...] = jnp.concatenate(per_head + per_head, axis=0) * hm_ref[...]
    m_ref[...] = jnp.full(m_ref.shape, NEG, _F32)
    l_ref[...] = jnp.zeros(l_ref.shape, _F32)
    acc_ref[...] = jnp.zeros(acc_ref.shape, _F32)

    def scores(grp, dst_ref, with_last_bias):
        slot = jnp.bitwise_and(grp, RING_GROUPS - 1)
        qm = qm_ref[...]
        for i in range(gp):
            s = lax.dot_general(qm, kbuf[slot * gp + i], _NT, preferred_element_type=_F32) + neg_ref[...]
            if with_last_bias and i == gp - 1:
                s = s + bl_ref[...]
            dst_ref[:, i * page_rows:(i + 1) * page_rows] = s

    def consume(grp, src_ref):
        slot = jnp.bitwise_and(grp, RING_GROUPS - 1)
        smax = src_ref[:, 0:page_rows]
        for i in range(1, gp):
            smax = jnp.maximum(smax, src_ref[:, i * page_rows:(i + 1) * page_rows])
        m_old = m_ref[...]
        m_new = jnp.maximum(m_old, jnp.max(smax, axis=-1, keepdims=True))
        alpha = jnp.exp2(m_old - m_new)
        l = alpha * l_ref[...]
        acc = alpha * acc_ref[...]
        for i in range(gp):
            p = jnp.exp2(src_ref[:, i * page_rows:(i + 1) * page_rows] - m_new)
            l = l + p
            acc = acc + jnp.dot(p, vbuf[slot * gp + i], preferred_element_type=_F32)
        l_ref[...] = l
        acc_ref[...] = acc
        m_ref[...] = m_new

    def block(grp, dst_ref, src_ref, with_last_bias=False):
        wait_group(ck_ref, kbuf, ksem, grp)
        wait_group(cv_ref, vbuf, vsem, grp - 1)
        scores(grp, dst_ref, with_last_bias)
        consume(grp - 1, src_ref)
        start_ahead(ck_ref, kbuf, ksem, grp)
        start_ahead(cv_ref, vbuf, vsem, grp - 1)

    wait_group(ck_ref, kbuf, ksem, 0)
    scores(0, s0_ref, False)
    start_ahead(ck_ref, kbuf, ksem, 0)

    def pair(i, carry):
        g = 2 * i + 1
        block(g, s1_ref, s0_ref)
        block(g + 1, s0_ref, s1_ref)
        return carry

    lax.fori_loop(0, (n_groups - 2) // 2, pair, 0)
    block(n_groups - 1, s1_ref, s0_ref, with_last_bias=True)
    wait_group(cv_ref, vbuf, vsem, n_groups - 1)
    consume(n_groups - 1, s1_ref)
    start_ahead(cv_ref, vbuf, vsem, n_groups - 1)

    sn = lax.dot_general(qm_ref[...], kn_ref[0], _NT, preferred_element_type=_F32) + bn_ref[...]
    m_o = m_ref[...]
    m_f = jnp.maximum(m_o, jnp.max(sn, axis=-1, keepdims=True))
    al = jnp.exp2(m_o - m_f)
    pn = jnp.exp2(sn - m_f)
    l_tot = al * jnp.sum(l_ref[...], axis=-1, keepdims=True) + jnp.sum(pn, axis=-1, keepdims=True)
    acc = al * acc_ref[...] + jnp.dot(pn, vn_ref[0], preferred_element_type=_F32)
    o_full = acc * (1.0 / l_tot)
    half = n_rows // 2
    od = o_full[0:half] - lam_ref[0, 0] * o_full[half:n_rows]
    ga = ga_ref[0]
    for h in range(H_A):
        cols = slice(h * DK, (h + 1) * DK)
        oh = _rms_norm_rows(od[h * T_PAD:(h + 1) * T_PAD], sg_ref[...]) * (1.0 - lam_init)
        a_ref[0, :, cols] = (oh * _silu(ga[:, cols])).astype(_BF16)


def _sample_attn(page_table, lam, q8, kn, vn, ga8, halfmask, negmask, bias_last, bias_new, subln_g, ck, cv,
                 lam_init):
    db, n_pages = page_table.shape
    gp = PAGES_PER_GROUP
    n_groups = n_pages // gp
    assert n_pages % gp == 0 and n_groups % RING_GROUPS == 0 and n_groups % 2 == 0
    page_rows = ck.shape[1]
    n_rows = 2 * H_A * T_PAD
    seq = lambda bi, pt: (bi, 0, 0)
    const = lambda bi, pt: (0, 0)
    grid_spec = pltpu.PrefetchScalarGridSpec(
        num_scalar_prefetch=1,
        grid=(db,),
        in_specs=[pl.BlockSpec(memory_space=pltpu.SMEM),
                  pl.BlockSpec((1, T_PAD, W_A), seq),
                  pl.BlockSpec((1, T_PAD * H_A, DK), seq), pl.BlockSpec((1, T_PAD * H_A, DK), seq),
                  pl.BlockSpec((1, T_PAD, W_A), seq),
                  pl.BlockSpec(halfmask.shape, const), pl.BlockSpec(negmask.shape, const),
                  pl.BlockSpec(bias_last.shape, const), pl.BlockSpec(bias_new.shape, const),
                  pl.BlockSpec(subln_g.shape, const),
                  pl.BlockSpec(memory_space=pl.ANY), pl.BlockSpec(memory_space=pl.ANY)],
        out_specs=pl.BlockSpec((1, T_PAD, W_A), seq),
        scratch_shapes=[pltpu.VMEM((RING_GROUPS * gp, page_rows, DK), _F32),
                        pltpu.VMEM((RING_GROUPS * gp, page_rows, DK), _F32),
                        pltpu.SemaphoreType.DMA((RING_GROUPS,)),
                        pltpu.SemaphoreType.DMA((RING_GROUPS,)),
                        pltpu.VMEM((n_rows, DK), _F32),
                        pltpu.VMEM((n_rows, gp * page_rows), _F32),
                        pltpu.VMEM((n_rows, gp * page_rows), _F32),
                        pltpu.VMEM((n_rows, 1), _F32),
                        pltpu.VMEM((n_rows, page_rows), _F32),
                        pltpu.VMEM((n_rows, DK), _F32)],
    )
    return pl.pallas_call(
        functools.partial(_sample_attn_kernel, lam_init=lam_init, n_groups=n_groups, n_seq=db),
        grid_spec=grid_spec,
        out_shape=jax.ShapeDtypeStruct((db, T_PAD, W_A), _BF16),
        compiler_params=pltpu.CompilerParams(dimension_semantics=("arbitrary",),
                                             vmem_limit_bytes=VMEM_LIMIT),
        name="sample_attn",
    )(page_table, lam, q8, kn, vn, ga8, halfmask, negmask, bias_last, bias_new, subln_g, ck, cv)


def kernel(x_prompt, x_sample, cache_k, cache_v, page_table, w_in, w_out, lambda_q1, lambda_k1, lambda_q2,
           lambda_k2, subln_g, rel_bias, sgu_ln_g, sgu_ln_b, sgu_w, sgu_b, post_ln_g, post_ln_b):
    depth = w_in.shape[0]
    assert depth == 1, "single trunk layer"
    b, s, d = x_prompt.shape
    db, t, _ = x_sample.shape
    n_pool, page = cache_k.shape[1], cache_k.shape[2]
    assert s % TQ == 0 and s % TM == 0 and TQ == TK and TQ >= MAX_DISTANCE
    assert t <= T_PAD and page >= MAX_DISTANCE
    lam_init = 0.8 - 0.6 * math.exp(-0.3 * 0)
    alpha = (2.0 * depth) ** 0.25

    col_scale = jnp.where(jnp.arange(w_in.shape[2]) < W_A, SCALE * LOG2E, 1.0).astype(_F32)
    w_all = (w_in[0] * col_scale[None, :]).astype(_BF16)
    wo_a = w_out[0, 0:W_A].astype(_BF16)
    wo_g = w_out[0, W_A:].astype(_BF16)
    ln_g = sgu_ln_g[0][None]
    ln_b = sgu_ln_b[0][None]
    pg_ = post_ln_g[0][None]
    pb_ = post_ln_b[0][None]
    sg = subln_g[0][None]
    tril = jnp.tril(jnp.ones((CHUNK, CHUNK), _F32))
    wt = (sgu_w[0] * tril).astype(_BF16)
    bsb = jnp.broadcast_to(sgu_b[0][:, :, None], (H_B, CHUNK, CB))
    wsmall = sgu_w[0][:, :T_PAD, :T_PAD] * tril[:T_PAD, :T_PAD]
    valid_t = (jnp.arange(T_PAD) < t).astype(_F32)
    cs = jnp.transpose(wsmall, (2, 1, 0)) * valid_t[None, :, None]
    cs = jnp.repeat(cs, CB, axis=2)
    bs8 = jnp.repeat(jnp.transpose(sgu_b[0][:, :T_PAD]) * valid_t[:, None], CB, axis=1)
    lam_params = jnp.concatenate([lambda_q1, lambda_k1, lambda_q2, lambda_k2], axis=0)

    kk = np.arange(TK)[:, None]
    qq = np.arange(TQ)[None, :]
    idx_sub = _bucket_np(TQ + qq - kk)
    idx_diag = np.where(qq >= kk, _bucket_np(qq - kk), MASKED_BUCKET)
    bias_prompt = _bias_tiles(rel_bias, jnp.asarray(np.stack([idx_sub, idx_diag]).astype(np.int32)))
    tt = np.arange(T_PAD)[:, None]
    ii = np.arange(page * H_A)[None, :] // H_A
    idx_last = _bucket_np(np.where(tt < t, page + tt - ii, MAX_DISTANCE))
    idx_new = np.where((ii <= tt) & (tt < t), _bucket_np(tt - ii), MASKED_BUCKET)
    bias_sample = _bias_tiles(rel_bias, jnp.asarray(np.stack([idx_last, idx_new]).astype(np.int32)))
    r = np.arange(2 * H_A * T_PAD)
    r_head = (r // T_PAD) % H_A
    r_map = r // (H_A * T_PAD)
    lane_head = np.arange(page * H_A) % H_A
    own_head = r_head[:, None] == lane_head[None, :]
    negmask = jnp.asarray(np.where(own_head, 0.0, NEG).astype(np.float32))
    halfmask = jnp.asarray((np.arange(DK)[None, :] // HD == r_map[:, None]).astype(np.float32))
    bias_last = jnp.tile(bias_sample[:, 0].reshape(H_A * T_PAD, page * H_A), (2, 1))
    bias_new = (jnp.tile(bias_sample[:, 1, :, :T_PAD * H_A].reshape(H_A * T_PAD, T_PAD * H_A), (2, 1))
                + negmask[:, :T_PAD * H_A])

    xs = jnp.pad(x_sample, ((0, 0), (0, T_PAD - t), (0, 0))).reshape(db * T_PAD, d)
    q8, k8, v8, ga8, g8, vn8, lam_tile = _sample_proj(xs, w_all, ln_g, ln_b, cs, bs8, lam_params, lam_init, t)
    lam = lam_tile[0:1, 0:1]
    ck = cache_k[0].reshape(n_pool, page * H_A, DK)
    cv = cache_v[0].reshape(n_pool, page * H_A, DK)
    a8 = _sample_attn(page_table, lam, q8.reshape(db, T_PAD, W_A), k8.reshape(db, T_PAD * H_A, DK),
                      v8.reshape(db, T_PAD * H_A, DK), ga8.reshape(db, T_PAD, W_A), halfmask, negmask,
                      bias_last, bias_new, sg, ck, cv, lam_init)
    ys = _out_proj(a8.reshape(db * T_PAD, W_A), g8, xs, wo_a, wo_g, pg_, pb_, alpha)
    y_sample = ys.reshape(db, T_PAD, d)[:, :t]
    nks = k8.reshape(db, T_PAD, H_A, DK)[None, :, :t]
    nvs = v8.reshape(db, T_PAD, H_A, DK)[None, :, :t]
    nsv = vn8.reshape(db, T_PAD, H_B, CB)[None, :, :t]

    kf, vf, kb, qt, vt, ga, g = _prompt_proj(x_prompt, w_all, ln_g, ln_b, wt, bsb)
    a = _prompt_attn(lam, qt, kb, vt, bias_prompt, ga, sg, lam_init)
    yp = _out_proj(a.reshape(b * s, W_A), g.reshape(b * s, W_B), x_prompt.reshape(b * s, d),
                   wo_a, wo_g, pg_, pb_, alpha)
    y_prompt = yp.reshape(b, s, d)
    nkp = kf.reshape(1, b, s, H_A, DK)
    nvp = vf.reshape(1, b, s, H_A, DK)
    return (y_prompt, y_sample, nkp, nvp, nks, nvs, nsv)
```

```python
import functools
import math

import numpy as np
import jax
import jax.numpy as jnp
from jax import lax
from jax.experimental import pallas as pl
from jax.experimental.pallas import tpu as pltpu

H_A = 4
HD = 64
DK = 2 * HD
W_A = H_A * DK
H_B = 4
CB = 128
W_B = H_B * CB
CHUNK = 128
N_BUCKETS = 32
MAX_DISTANCE = 128
LN_EPS = 1e-5
SCALE = HD ** -0.5
LOG2E = math.log2(math.e)
NEG = -1e30
MASKED_BUCKET = N_BUCKETS

TM = 512
TM_OUT = 1024
OUT_SUB = 256
TQ = 512
TK = 512
QH = TQ // 2
ONES_ROWS = 16
VA = DK + ONES_ROWS
T_PAD = 8
PAGES_PER_GROUP = 8
RING_GROUPS = 8
VMEM_LIMIT = 56 * 1024 * 1024

_F32 = jnp.float32
_BF16 = jnp.bfloat16
_NT = (((1,), (1,)), ((), ()))


def _bucket_np(dist):
    n = np.maximum(dist, 0)
    max_exact = N_BUCKETS // 2
    nf = np.maximum(n, 1).astype(np.float32)
    large = max_exact + (np.log(nf / max_exact) / math.log(MAX_DISTANCE / max_exact)
                         * (N_BUCKETS - max_exact)).astype(np.int32)
    large = np.minimum(large, N_BUCKETS - 1)
    return np.where(n < max_exact, n, large).astype(np.int32)


def _silu(x):
    return x * jax.nn.sigmoid(x)


def _layer_norm_rows(x, g, b):
    mu = jnp.mean(x, axis=-1, keepdims=True)
    xc = x - mu
    var = jnp.mean(xc * xc, axis=-1, keepdims=True)
    return xc * lax.rsqrt(var + LN_EPS) * g + b


def _rms_norm_rows(x, g):
    return x * lax.rsqrt(jnp.mean(x * x, axis=-1, keepdims=True) + LN_EPS) * g


def _bias_kernel(tab_ref, idx_ref, o_ref):
    h = pl.program_id(0)
    idx = idx_ref[0]
    far = tab_ref[N_BUCKETS - 1, h]
    acc = jnp.full(idx.shape, NEG, _F32)
    for b in range(N_BUCKETS):
        acc = jnp.where(idx == b, (tab_ref[b, h] - far) * LOG2E, acc)
    o_ref[0, 0] = acc


def _bias_tiles(table, idx):
    n, r, c = idx.shape
    return pl.pallas_call(
        _bias_kernel,
        grid=(H_A, n),
        in_specs=[pl.BlockSpec(memory_space=pltpu.SMEM),
                  pl.BlockSpec((1, r, c), lambda h, i: (i, 0, 0))],
        out_specs=pl.BlockSpec((1, 1, r, c), lambda h, i: (h, i, 0, 0)),
        out_shape=jax.ShapeDtypeStruct((H_A, n, r, c), _F32),
        name="bias_tiles",
    )(table, idx)


def _prompt_proj_kernel(x_ref, w_ref, lng_ref, lnb_ref, wt_ref, bs_ref,
                        kf_ref, vf_ref, kb_ref, qt_ref, vt_ref, ga_ref, g_ref):
    xb = x_ref[0].astype(_BF16)
    hm = jnp.dot(xb, w_ref[...], preferred_element_type=_F32)
    k = hm[:, W_A:2 * W_A]
    v = hm[:, 2 * W_A:3 * W_A]
    for h in range(H_A):
        cols = slice(h * DK, (h + 1) * DK)
        kf_ref[0, pl.ds(h, TM, stride=H_A), :] = k[:, cols]
        vf_ref[0, pl.ds(h, TM, stride=H_A), :] = v[:, cols]
    kb_ref[0] = k.astype(_BF16)
    ga_ref[0] = hm[:, 3 * W_A:4 * W_A]
    qt_ref[0] = hm[:, 0:W_A].T.astype(_BF16)
    vt = v.T.astype(_BF16)
    for h in range(H_A):
        vt_ref[0, h, 0:DK, :] = vt[h * DK:(h + 1) * DK]
        vt_ref[0, h, DK:VA, :] = jnp.ones((ONES_ROWS, TM), _BF16)
    o = 4 * W_A
    u = hm[:, o:o + W_B]
    vb = hm[:, o + W_B:o + 2 * W_B]
    gb = hm[:, o + 2 * W_B:o + 3 * W_B]
    vn = _layer_norm_rows(vb, lng_ref[...], lnb_ref[...]).astype(_BF16)
    gate = u * _silu(gb)
    n_chunks = TM // CHUNK
    for h in range(H_B):
        cols = slice(h * CB, (h + 1) * CB)
        vn_h = jnp.concatenate([vn[c * CHUNK:(c + 1) * CHUNK, cols] for c in range(n_chunks)], axis=1)
        s = jnp.dot(wt_ref[h], vn_h, preferred_element_type=_F32)
        for c in range(n_chunks):
            rows = slice(c * CHUNK, (c + 1) * CHUNK)
            g_ref[0, rows, cols] = (gate[rows, cols] * (s[:, c * CB:(c + 1) * CB] + bs_ref[h])).astype(_BF16)


def _prompt_proj(x, w_all, ln_g, ln_b, wt, bsb):
    b, s, d = x.shape
    row = lambda bi, si: (bi, si, 0)
    col = lambda bi, si: (bi, 0, si)
    const2 = lambda bi, si: (0, 0)
    const3 = lambda bi, si: (0, 0, 0)
    f32_heads = jax.ShapeDtypeStruct((b, s * H_A, DK), _F32)
    bf_rows = jax.ShapeDtypeStruct((b, s, W_A), _BF16)
    bf_cols = jax.ShapeDtypeStruct((b, W_A, s), _BF16)
    return pl.pallas_call(
        _prompt_proj_kernel,
        grid=(b, s // TM),
        in_specs=[pl.BlockSpec((1, TM, d), row),
                  pl.BlockSpec(w_all.shape, const2),
                  pl.BlockSpec(ln_g.shape, const2),
                  pl.BlockSpec(ln_b.shape, const2),
                  pl.BlockSpec(wt.shape, const3),
                  pl.BlockSpec(bsb.shape, const3)],
        out_specs=[pl.BlockSpec((1, TM * H_A, DK), row), pl.BlockSpec((1, TM * H_A, DK), row),
                   pl.BlockSpec((1, TM, W_A), row),
                   pl.BlockSpec((1, W_A, TM), col),
                   pl.BlockSpec((1, H_A, VA, TM), lambda bi, si: (bi, 0, 0, si)),
                   pl.BlockSpec((1, TM, W_A), row), pl.BlockSpec((1, TM, W_B), row)],
        out_shape=[f32_heads, f32_heads, bf_rows, bf_cols,
                   jax.ShapeDtypeStruct((b, H_A, VA, s), _BF16),
                   jax.ShapeDtypeStruct((b, s, W_A), _F32), jax.ShapeDtypeStruct((b, s, W_B), _BF16)],
        compiler_params=pltpu.CompilerParams(dimension_semantics=("arbitrary", "arbitrary"),
                                             vmem_limit_bytes=VMEM_LIMIT),
        name="prompt_proj",
    )(x, w_all, ln_g, ln_b, wt, bsb)


def _prompt_attn_kernel(*refs, lam_init, n_q):
    def body(qi, carry):
        _prompt_attn_block(qi, n_q, *refs, lam_init=lam_init)
        return carry

    lax.fori_loop(0, n_q, body, 0)


def _prompt_attn_block(qi, n_q, lam_ref, qt_ref, k_ref, vt_ref, bias_ref, ga_ref, sg_ref, a_ref,
                       qz_ref, qzn_ref, sa_ref, sb_ref, sc_ref, mxa_ref, mxb_ref, mxc_ref, m_ref, l_ref, acc_ref,
                       *, lam_init):
    q_rows = pl.ds(pl.multiple_of(qi * TQ, TQ), TQ)

    def split_maps(qt, dst_ref):
        row = lax.broadcasted_iota(jnp.int32, qt.shape, 0)
        zero = jnp.zeros_like(qt)
        dst_ref[:, 0:TQ] = jnp.where(row < HD, qt, zero)
        dst_ref[:, TQ:2 * TQ] = jnp.where(row >= HD, qt, zero)

    def begin():
        split_maps(qt_ref[0, :, q_rows], qz_ref)
        m_ref[...] = jnp.full(m_ref.shape, NEG, _F32)
        l_ref[...] = jnp.zeros(l_ref.shape, _F32)
        acc_ref[...] = jnp.zeros(acc_ref.shape, _F32)

    def finish():
        lam = lam_ref[0, 0]
        o_t = acc_ref[0] * (1.0 / l_ref[0]) - lam * (acc_ref[1] * (1.0 / l_ref[1]))
        o = o_t.T
        o = _rms_norm_rows(o, sg_ref[...]) * (1.0 - lam_init)
        a_ref[0, q_rows, :] = (o * _silu(ga_ref[0, q_rows, :])).astype(_BF16)

    def scores(j, dst_ref, mx_ref, bias_idx=None, q_ref=qz_ref):
        start = pl.multiple_of(j * TK, TK)
        s = jnp.dot(k_ref[0, pl.ds(start, TK), :], q_ref[...], preferred_element_type=_F32)
        if bias_idx == 1:
            bias = bias_ref[0, 1]
            s = jnp.concatenate([s[:, 0:TQ] + bias, s[:, TQ:2 * TQ] + bias], axis=1)
        elif bias_idx == 0:
            md = MAX_DISTANCE
            corner = bias_ref[0, 0, TK - md:TK, 0:md]
            low = s[TK - md:TK]
            low = jnp.concatenate([low[:, 0:md] + corner, low[:, md:TQ],
                                   low[:, TQ:TQ + md] + corner, low[:, TQ + md:2 * TQ]], axis=1)
            s = jnp.concatenate([s[0:TK - md], low], axis=0)
        dst_ref[...] = s
        mx_ref[...] = jnp.max(s, axis=0, keepdims=True)

    def consume(src_ref, mx_ref, j, diagonal=False):
        start = pl.multiple_of(j * TK, TK)
        vblk = vt_ref[0, 0, :, pl.ds(start, TK)]
        for mp in range(2):
            for hf in range(TQ // QH):
                qs = slice(hf * QH, (hf + 1) * QH)
                cols = slice(mp * TQ + hf * QH, mp * TQ + (hf + 1) * QH)
                n_keys = (hf + 1) * QH if diagonal else TK
                m_old = m_ref[mp, :, qs]
                m_new = jnp.maximum(m_old, mx_ref[:, cols])
                alpha = jnp.exp2(m_old - m_new)
                p = jnp.exp2(src_ref[0:n_keys, cols] - m_new).astype(_BF16)
                pv = jnp.dot(vblk[:, 0:n_keys], p, preferred_element_type=_F32)
                acc_ref[mp, :, qs] = acc_ref[mp, :, qs] * alpha + pv[0:DK]
                l_ref[mp, :, qs] = l_ref[mp, :, qs] * alpha + pv[DK:DK + 1]
                m_ref[mp, :, qs] = m_new

    def next_diagonal():
        jn = jnp.minimum(qi + 1, n_q - 1)
        split_maps(qt_ref[0, :, pl.ds(pl.multiple_of(jn * TQ, TQ), TQ)], qzn_ref)
        scores(jn, sc_ref, mxc_ref, bias_idx=1, q_ref=qzn_ref)

    n_far = jnp.maximum(qi - 1, 0)
    n_rest = jnp.maximum(n_far - 1, 0)
    odd_rest = jnp.bitwise_and(n_rest, 1)

    @pl.when(qi == 0)
    def _():
        begin()
        scores(0, sc_ref, mxc_ref, bias_idx=1)
        consume(sc_ref, mxc_ref, 0, diagonal=True)
        finish()

    @pl.when(qi == 1)
    def _():
        begin()
        scores(1, sc_ref, mxc_ref, bias_idx=1)
        scores(0, sb_ref, mxb_ref, bias_idx=0)
        consume(sc_ref, mxc_ref, 1, diagonal=True)
        consume(sb_ref, mxb_ref, 0)
        next_diagonal()
        finish()

    @pl.when(qi >= 2)
    def _():
        begin()
        scores(qi - 1, sb_ref, mxb_ref, bias_idx=0)
        consume(sc_ref, mxc_ref, qi, diagonal=True)
        scores(0, sa_ref, mxa_ref)
        consume(sb_ref, mxb_ref, qi - 1)

    def far_tiles(f0, count):
        for t in range(0, count, 2):
            scores(f0 + t + 1, sb_ref, mxb_ref)
            consume(sa_ref, mxa_ref, f0 + t)
            scores(f0 + t + 2, sa_ref, mxa_ref)
            consume(sb_ref, mxb_ref, f0 + t + 1)

    def far_quad(i, carry):
        far_tiles(4 * i, 4)
        return carry

    n_quads = lax.shift_right_logical(n_rest, 2)
    lax.fori_loop(0, n_quads, far_quad, 0)

    @pl.when(jnp.bitwise_and(n_rest, 2) == 2)
    def _():
        far_tiles(4 * n_quads, 2)

    @pl.when(jnp.logical_and(qi >= 2, odd_rest == 0))
    def _():
        consume(sa_ref, mxa_ref, n_far - 1)
        next_diagonal()
        finish()

    @pl.when(jnp.logical_and(qi >= 2, odd_rest == 1))
    def _():
        scores(n_far - 1, sb_ref, mxb_ref)
        consume(sa_ref, mxa_ref, n_far - 2)
        consume(sb_ref, mxb_ref, n_far - 1)
        next_diagonal()
        finish()


def _prompt_attn(lam, qt, kb, vt, bias, ga, subln_g, lam_init):
    b, s, _ = kb.shape
    return pl.pallas_call(
        functools.partial(_prompt_attn_kernel, lam_init=lam_init, n_q=s // TQ),
        grid=(b, H_A),
        in_specs=[pl.BlockSpec(memory_space=pltpu.SMEM),
                  pl.BlockSpec((1, DK, s), lambda bi, h: (bi, h, 0)),
                  pl.BlockSpec((1, s, DK), lambda bi, h: (bi, 0, h)),
                  pl.BlockSpec((1, 1, VA, s), lambda bi, h: (bi, h, 0, 0)),
                  pl.BlockSpec((1, 2, TK, TQ), lambda bi, h: (h, 0, 0, 0)),
                  pl.BlockSpec((1, s, DK), lambda bi, h: (bi, 0, h)),
                  pl.BlockSpec(subln_g.shape, lambda bi, h: (0, 0))],
        out_specs=pl.BlockSpec((1, s, DK), lambda bi, h: (bi, 0, h)),
        out_shape=jax.ShapeDtypeStruct((b, s, W_A), _BF16),
        scratch_shapes=[pltpu.VMEM((DK, 2 * TQ), _BF16),
                        pltpu.VMEM((DK, 2 * TQ), _BF16),
                        pltpu.VMEM((TK, 2 * TQ), _F32),
                        pltpu.VMEM((TK, 2 * TQ), _F32),
                        pltpu.VMEM((TK, 2 * TQ), _F32),
                        pltpu.VMEM((1, 2 * TQ), _F32),
                        pltpu.VMEM((1, 2 * TQ), _F32),
                        pltpu.VMEM((1, 2 * TQ), _F32),
                        pltpu.VMEM((2, 1, TQ), _F32),
                        pltpu.VMEM((2, 1, TQ), _F32),
                        pltpu.VMEM((2, DK, TQ), _F32)],
        compiler_params=pltpu.CompilerParams(dimension_semantics=("arbitrary", "arbitrary"),
                                             vmem_limit_bytes=VMEM_LIMIT),
        name="prompt_attn",
    )(lam, qt, kb, vt, bias, ga, subln_g)


def _out_proj_kernel(a_ref, g_ref, x_ref, wa_ref, wg_ref, lng_ref, lnb_ref, y_ref, *, alpha):
    sub = min(OUT_SUB, a_ref.shape[0])
    for i in range(a_ref.shape[0] // sub):
        rows = slice(i * sub, (i + 1) * sub)
        out = (jnp.dot(a_ref[rows, :], wa_ref[...], preferred_element_type=_F32)
               + jnp.dot(g_ref[rows, :], wg_ref[...], preferred_element_type=_F32))
        z = alpha * x_ref[rows, :] + out
        y_ref[rows, :] = _layer_norm_rows(z, lng_ref[...], lnb_ref[...])


def _out_proj(a, g, x, wo_a, wo_g, ln_g, ln_b, alpha):
    n, d = x.shape
    tm = min(n, TM_OUT)
    assert n % tm == 0
    row = lambda i: (i, 0)
    const = lambda i: (0, 0)
    return pl.pallas_call(
        functools.partial(_out_proj_kernel, alpha=alpha),
        grid=(n // tm,),
        in_specs=[pl.BlockSpec((tm, W_A), row), pl.BlockSpec((tm, W_B), row), pl.BlockSpec((tm, d), row),
                  pl.BlockSpec(wo_a.shape, const), pl.BlockSpec(wo_g.shape, const),
                  pl.BlockSpec(ln_g.shape, const), pl.BlockSpec(ln_b.shape, const)],
        out_specs=pl.BlockSpec((tm, d), row),
        out_shape=jax.ShapeDtypeStruct((n, d), _F32),
        compiler_params=pltpu.CompilerParams(dimension_semantics=("arbitrary",),
                                             vmem_limit_bytes=VMEM_LIMIT),
        name="out_proj",
    )(a, g, x, wo_a, wo_g, ln_g, ln_b)


def _sample_proj_kernel(x_ref, w_ref, lng_ref, lnb_ref, cs_ref, bs_ref, lp_ref,
                        q_ref, k_ref, v_ref, ga_ref, g_ref, vn_ref, lam_ref, *, lam_init, t_valid):
    rows = x_ref.shape[0]
    h = jnp.dot(x_ref[...].astype(_BF16), w_ref[...], preferred_element_type=_F32)
    q_ref[...] = h[:, 0:W_A]
    k_ref[...] = h[:, W_A:2 * W_A]
    v_ref[...] = h[:, 2 * W_A:3 * W_A]
    ga_ref[...] = h[:, 3 * W_A:4 * W_A]
    o = 4 * W_A
    u = h[:, o:o + W_B]
    vb = h[:, o + W_B:o + 2 * W_B]
    gb = h[:, o + 2 * W_B:o + 3 * W_B]
    vn = _layer_norm_rows(vb, lng_ref[...], lnb_ref[...])
    vn_ref[...] = vn
    vn3 = vn.reshape(rows // T_PAD, T_PAD, W_B)
    s3 = jnp.broadcast_to(bs_ref[...][None], vn3.shape)
    for j in range(t_valid):
        s3 = s3 + vn3[:, j:j + 1, :] * cs_ref[j][None]
    g_ref[...] = (u * s3.reshape(rows, W_B) * _silu(gb)).astype(_BF16)
    lp = lp_ref[...]
    e1 = jnp.exp(jnp.sum(lp[0:1] * lp[1:2], axis=-1, keepdims=True))
    e2 = jnp.exp(jnp.sum(lp[2:3] * lp[3:4], axis=-1, keepdims=True))
    lam_ref[...] = jnp.broadcast_to(e1 - e2 + lam_init, lam_ref.shape)


def _sample_proj(x, w, ln_g, ln_b, cs, bs8, lam_params, lam_init, t_valid):
    n, d = x.shape
    tm = min(n, 256)
    row = lambda i: (i, 0)
    const2 = lambda i: (0, 0)
    const3 = lambda i: (0, 0, 0)
    f32_rows = jax.ShapeDtypeStruct((n, W_A), _F32)
    return pl.pallas_call(
        functools.partial(_sample_proj_kernel, lam_init=lam_init, t_valid=t_valid),
        grid=(n // tm,),
        in_specs=[pl.BlockSpec((tm, d), row), pl.BlockSpec(w.shape, const2),
                  pl.BlockSpec(ln_g.shape, const2), pl.BlockSpec(ln_b.shape, const2),
                  pl.BlockSpec(cs.shape, const3), pl.BlockSpec(bs8.shape, const2),
                  pl.BlockSpec(lam_params.shape, const2)],
        out_specs=[pl.BlockSpec((tm, W_A), row)] * 4
                  + [pl.BlockSpec((tm, W_B), row), pl.BlockSpec((tm, W_B), row),
                     pl.BlockSpec((8, 128), const2)],
        out_shape=[f32_rows] * 4 + [jax.ShapeDtypeStruct((n, W_B), _BF16),
                                    jax.ShapeDtypeStruct((n, W_B), _F32),
                                    jax.ShapeDtypeStruct((8, 128), _F32)],
        compiler_params=pltpu.CompilerParams(dimension_semantics=("arbitrary",),
                                             vmem_limit_bytes=VMEM_LIMIT),
        name="sample_proj",
    )(x, w, ln_g, ln_b, cs, bs8, lam_params)


def _sample_attn_kernel(pt_ref, lam_ref, q_ref, kn_ref, vn_ref, ga_ref, hm_ref, neg_ref, bl_ref, bn_ref, sg_ref,
                        ck_ref, cv_ref, a_ref, kbuf, vbuf, ksem, vsem, qm_ref, s0_ref, s1_ref, m_ref, l_ref, acc_ref,
                        *, lam_init, n_groups, n_seq):
    gp = PAGES_PER_GROUP
    b = pl.program_id(0)
    n_rows = 2 * H_A * T_PAD
    page_rows = kbuf.shape[1]

    def group_copies(hbm_ref, buf, sem, seq, grp):
        slot = jnp.bitwise_and(grp, RING_GROUPS - 1)
        return [pltpu.make_async_copy(hbm_ref.at[pt_ref[seq, grp * gp + i]], buf.at[slot * gp + i], sem.at[slot])
                for i in range(gp)]

    def start_ahead(hbm_ref, buf, sem, grp):
        ahead = jnp.asarray(grp + RING_GROUPS, jnp.int32)
        wraps = (ahead >= n_groups).astype(jnp.int32)
        seq = b + wraps
        grp2 = ahead - wraps * n_groups

        @pl.when(seq < n_seq)
        def _():
            for i, c in enumerate(group_copies(hbm_ref, buf, sem, seq, grp2)):
                c.start(priority=i % 2)

    def wait_group(hbm_ref, buf, sem, grp):
        for c in group_copies(hbm_ref, buf, sem, b, grp):
            c.wait()

    @pl.when(b == 0)
    def _():
        for g in range(RING_GROUPS):
            for i, c in enumerate(group_copies(ck_ref, kbuf, ksem, 0, g) + group_copies(cv_ref, vbuf, vsem, 0, g)):
                c.start(priority=i % 2)

    q8 = q_ref[0]
    per_head = [q8[:, h * DK:(h + 1) * DK] for h in range(H_A)]
    qm_ref[...] = jnp.concatenate(per_head + per_head, axis=0) * hm_ref[...]
    m_ref[...] = jnp.full(m_ref.shape, NEG, _F32)
    l_ref[...] = jnp.zeros(l_ref.shape, _F32)
    acc_ref[...] = jnp.zeros(acc_ref.shape, _F32)

    def scores(grp, dst_ref, with_last_bias):
        slot = jnp.bitwise_and(grp, RING_GROUPS - 1)
        qm = qm_ref[...]
        for i in range(gp):
            s = lax.dot_general(qm, kbuf[slot * gp + i], _NT, preferred_element_type=_F32) + neg_ref[...]
            if with_last_bias and i == gp - 1:
                s = s + bl_ref[...]
            dst_ref[:, i * page_rows:(i + 1) * page_rows] = s

    def consume(grp, src_ref):
        slot = jnp.bitwise_and(grp, RING_GROUPS - 1)
        smax = src_ref[:, 0:page_rows]
        for i in range(1, gp):
            smax = jnp.maximum(smax, src_ref[:, i * page_rows:(i + 1) * page_rows])
        m_old = m_ref[...]
        m_new = jnp.maximum(m_old, jnp.max(smax, axis=-1, keepdims=True))
        alpha = jnp.exp2(m_old - m_new)
        l = alpha * l_ref[...]
        acc = alpha * acc_ref[...]
        for i in range(gp):
            p = jnp.exp2(src_ref[:, i * page_rows:(i + 1) * page_rows] - m_new)
            l = l + p
            acc = acc + jnp.dot(p, vbuf[slot * gp + i], preferred_element_type=_F32)
        l_ref[...] = l
        acc_ref[...] = acc
        m_ref[...] = m_new

    def block(grp, dst_ref, src_ref, with_last_bias=False):
        wait_group(ck_ref, kbuf, ksem, grp)
        wait_group(cv_ref, vbuf, vsem, grp - 1)
        scores(grp, dst_ref, with_last_bias)
        consume(grp - 1, src_ref)
        start_ahead(ck_ref, kbuf, ksem, grp)
        start_ahead(cv_ref, vbuf, vsem, grp - 1)

    wait_group(ck_ref, kbuf, ksem, 0)
    scores(0, s0_ref, False)
    start_ahead(ck_ref, kbuf, ksem, 0)

    def pair(i, carry):
        g = 2 * i + 1
        block(g, s1_ref, s0_ref)
        block(g + 1, s0_ref, s1_ref)
        return carry

    lax.fori_loop(0, (n_groups - 2) // 2, pair, 0)
    block(n_groups - 1, s1_ref, s0_ref, with_last_bias=True)
    wait_group(cv_ref, vbuf, vsem, n_groups - 1)
    consume(n_groups - 1, s1_ref)
    start_ahead(cv_ref, vbuf, vsem, n_groups - 1)

    sn = lax.dot_general(qm_ref[...], kn_ref[0], _NT, preferred_element_type=_F32) + bn_ref[...]
    m_o = m_ref[...]
    m_f = jnp.maximum(m_o, jnp.max(sn, axis=-1, keepdims=True))
    al = jnp.exp2(m_o - m_f)
    pn = jnp.exp2(sn - m_f)
    l_tot = al * jnp.sum(l_ref[...], axis=-1, keepdims=True) + jnp.sum(pn, axis=-1, keepdims=True)
    acc = al * acc_ref[...] + jnp.dot(pn, vn_ref[0], preferred_element_type=_F32)
    o_full = acc * (1.0 / l_tot)
    half = n_rows // 2
    od = o_full[0:half] - lam_ref[0, 0] * o_full[half:n_rows]
    ga = ga_ref[0]
    for h in range(H_A):
        cols = slice(h * DK, (h + 1) * DK)
        oh = _rms_norm_rows(od[h * T_PAD:(h + 1) * T_PAD], sg_ref[...]) * (1.0 - lam_init)
        a_ref[0, :, cols] = (oh * _silu(ga[:, cols])).astype(_BF16)


def _sample_attn(page_table, lam, q8, kn, vn, ga8, halfmask, negmask, bias_last, bias_new, subln_g, ck, cv,
                 lam_init):
    db, n_pages = page_table.shape
    gp = PAGES_PER_GROUP
    n_groups = n_pages // gp
    assert n_pages % gp == 0 and n_groups % RING_GROUPS == 0 and n_groups % 2 == 0
    page_rows = ck.shape[1]
    n_rows = 2 * H_A * T_PAD
    seq = lambda bi, pt: (bi, 0, 0)
    const = lambda bi, pt: (0, 0)
    grid_spec = pltpu.PrefetchScalarGridSpec(
        num_scalar_prefetch=1,
        grid=(db,),
        in_specs=[pl.BlockSpec(memory_space=pltpu.SMEM),
                  pl.BlockSpec((1, T_PAD, W_A), seq),
                  pl.BlockSpec((1, T_PAD * H_A, DK), seq), pl.BlockSpec((1, T_PAD * H_A, DK), seq),
                  pl.BlockSpec((1, T_PAD, W_A), seq),
                  pl.BlockSpec(halfmask.shape, const), pl.BlockSpec(negmask.shape, const),
                  pl.BlockSpec(bias_last.shape, const), pl.BlockSpec(bias_new.shape, const),
                  pl.BlockSpec(subln_g.shape, const),
                  pl.BlockSpec(memory_space=pl.ANY), pl.BlockSpec(memory_space=pl.ANY)],
        out_specs=pl.BlockSpec((1, T_PAD, W_A), seq),
        scratch_shapes=[pltpu.VMEM((RING_GROUPS * gp, page_rows, DK), _F32),
                        pltpu.VMEM((RING_GROUPS * gp, page_rows, DK), _F32),
                        pltpu.SemaphoreType.DMA((RING_GROUPS,)),
                        pltpu.SemaphoreType.DMA((RING_GROUPS,)),
                        pltpu.VMEM((n_rows, DK), _F32),
                        pltpu.VMEM((n_rows, gp * page_rows), _F32),
                        pltpu.VMEM((n_rows, gp * page_rows), _F32),
                        pltpu.VMEM((n_rows, 1), _F32),
                        pltpu.VMEM((n_rows, page_rows), _F32),
                        pltpu.VMEM((n_rows, DK), _F32)],
    )
    return pl.pallas_call(
        functools.partial(_sample_attn_kernel, lam_init=lam_init, n_groups=n_groups, n_seq=db),
        grid_spec=grid_spec,
        out_shape=jax.ShapeDtypeStruct((db, T_PAD, W_A), _BF16),
        compiler_params=pltpu.CompilerParams(dimension_semantics=("arbitrary",),
                                             vmem_limit_bytes=VMEM_LIMIT),
        name="sample_attn",
    )(page_table, lam, q8, kn, vn, ga8, halfmask, negmask, bias_last, bias_new, subln_g, ck, cv)


def kernel(x_prompt, x_sample, cache_k, cache_v, page_table, w_in, w_out, lambda_q1, lambda_k1, lambda_q2,
           lambda_k2, subln_g, rel_bias, sgu_ln_g, sgu_ln_b, sgu_w, sgu_b, post_ln_g, post_ln_b):
    depth = w_in.shape[0]
    assert depth == 1, "single trunk layer"
    b, s, d = x_prompt.shape
    db, t, _ = x_sample.shape
    n_pool, page = cache_k.shape[1], cache_k.shape[2]
    assert s % TQ == 0 and s % TM == 0 and TQ == TK and TQ >= MAX_DISTANCE
    assert t <= T_PAD and page >= MAX_DISTANCE
    lam_init = 0.8 - 0.6 * math.exp(-0.3 * 0)
    alpha = (2.0 * depth) ** 0.25

    col_scale = jnp.where(jnp.arange(w_in.shape[2]) < W_A, SCALE * LOG2E, 1.0).astype(_F32)
    w_all = (w_in[0] * col_scale[None, :]).astype(_BF16)
    wo_a = w_out[0, 0:W_A].astype(_BF16)
    wo_g = w_out[0, W_A:].astype(_BF16)
    ln_g = sgu_ln_g[0][None]
    ln_b = sgu_ln_b[0][None]
    pg_ = post_ln_g[0][None]
    pb_ = post_ln_b[0][None]
    sg = subln_g[0][None]
    tril = jnp.tril(jnp.ones((CHUNK, CHUNK), _F32))
    wt = (sgu_w[0] * tril).astype(_BF16)
    bsb = jnp.broadcast_to(sgu_b[0][:, :, None], (H_B, CHUNK, CB))
    wsmall = sgu_w[0][:, :T_PAD, :T_PAD] * tril[:T_PAD, :T_PAD]
    valid_t = (jnp.arange(T_PAD) < t).astype(_F32)
    cs = jnp.transpose(wsmall, (2, 1, 0)) * valid_t[None, :, None]
    cs = jnp.repeat(cs, CB, axis=2)
    bs8 = jnp.repeat(jnp.transpose(sgu_b[0][:, :T_PAD]) * valid_t[:, None], CB, axis=1)
    lam_params = jnp.concatenate([lambda_q1, lambda_k1, lambda_q2, lambda_k2], axis=0)

    kk = np.arange(TK)[:, None]
    qq = np.arange(TQ)[None, :]
    idx_sub = _bucket_np(TQ + qq - kk)
    idx_diag = np.where(qq >= kk, _bucket_np(qq - kk), MASKED_BUCKET)
    bias_prompt = _bias_tiles(rel_bias, jnp.asarray(np.stack([idx_sub, idx_diag]).astype(np.int32)))
    tt = np.arange(T_PAD)[:, None]
    ii = np.arange(page * H_A)[None, :] // H_A
    idx_last = _bucket_np(np.where(tt < t, page + tt - ii, MAX_DISTANCE))
    idx_new = np.where((ii <= tt) & (tt < t), _bucket_np(tt - ii), MASKED_BUCKET)
    bias_sample = _bias_tiles(rel_bias, jnp.asarray(np.stack([idx_last, idx_new]).astype(np.int32)))
    r = np.arange(2 * H_A * T_PAD)
    r_head = (r // T_PAD) % H_A
    r_map = r // (H_A * T_PAD)
    lane_head = np.arange(page * H_A) % H_A
    own_head = r_head[:, None] == lane_head[None, :]
    negmask = jnp.asarray(np.where(own_head, 0.0, NEG).astype(np.float32))
    halfmask = jnp.asarray((np.arange(DK)[None, :] // HD == r_map[:, None]).astype(np.float32))
    bias_last = jnp.tile(bias_sample[:, 0].reshape(H_A * T_PAD, page * H_A), (2, 1))
    bias_new = (jnp.tile(bias_sample[:, 1, :, :T_PAD * H_A].reshape(H_A * T_PAD, T_PAD * H_A), (2, 1))
                + negmask[:, :T_PAD * H_A])

    xs = jnp.pad(x_sample, ((0, 0), (0, T_PAD - t), (0, 0))).reshape(db * T_PAD, d)
    q8, k8, v8, ga8, g8, vn8, lam_tile = _sample_proj(xs, w_all, ln_g, ln_b, cs, bs8, lam_params, lam_init, t)
    lam = lam_tile[0:1, 0:1]
    ck = cache_k[0].reshape(n_pool, page * H_A, DK)
    cv = cache_v[0].reshape(n_pool, page * H_A, DK)
    a8 = _sample_attn(page_table, lam, q8.reshape(db, T_PAD, W_A), k8.reshape(db, T_PAD * H_A, DK),
                      v8.reshape(db, T_PAD * H_A, DK), ga8.reshape(db, T_PAD, W_A), halfmask, negmask,
                      bias_last, bias_new, sg, ck, cv, lam_init)
    ys = _out_proj(a8.reshape(db * T_PAD, W_A), g8, xs, wo_a, wo_g, pg_, pb_, alpha)
    y_sample = ys.reshape(db, T_PAD, d)[:, :t]
    nks = k8.reshape(db, T_PAD, H_A, DK)[None, :, :t]
    nvs = v8.reshape(db, T_PAD, H_A, DK)[None, :, :t]
    nsv = vn8.reshape(db, T_PAD, H_B, CB)[None, :, :t]

    kf, vf, kb, qt, vt, ga, g = _prompt_proj(x_prompt, w_all, ln_g, ln_b, wt, bsb)
    a = _prompt_attn(lam, qt, kb, vt, bias_prompt, ga, sg, lam_init)
    yp = _out_proj(a.reshape(b * s, W_A), g.reshape(b * s, W_B), x_prompt.reshape(b * s, d),
                   wo_a, wo_g, pg_, pb_, alpha)
    y_prompt = yp.reshape(b, s, d)
    nkp = kf.reshape(1, b, s, H_A, DK)
    nvp = vf.reshape(1, b, s, H_A, DK)
    return (y_prompt, y_sample, nkp, nvp, nks, nvs, nsv)
```

```python
import functools
import math

import numpy as np
import jax
import jax.numpy as jnp
from jax import lax
from jax.experimental import pallas as pl
from jax.experimental.pallas import tpu as pltpu

H_A = 4
HD = 64
DK = 2 * HD
W_A = H_A * DK
H_B = 4
CB = 128
W_B = H_B * CB
CHUNK = 128
N_BUCKETS = 32
MAX_DISTANCE = 128
LN_EPS = 1e-5
SCALE = HD ** -0.5
LOG2E = math.log2(math.e)
NEG = -1e30
MASKED_BUCKET = N_BUCKETS

TM = 512
TM_OUT = 2048
OUT_SUB = 256
TQ = 512
TK = 512
QH = TQ // 2
ONES_ROWS = 16
VA = DK + ONES_ROWS
T_PAD = 8
PAGES_PER_GROUP = 8
RING_GROUPS = 8
VMEM_LIMIT = 56 * 1024 * 1024

_F32 = jnp.float32
_BF16 = jnp.bfloat16
_NT = (((1,), (1,)), ((), ()))


def _bucket_np(dist):
    n = np.maximum(dist, 0)
    max_exact = N_BUCKETS // 2
    nf = np.maximum(n, 1).astype(np.float32)
    large = max_exact + (np.log(nf / max_exact) / math.log(MAX_DISTANCE / max_exact)
                         * (N_BUCKETS - max_exact)).astype(np.int32)
    large = np.minimum(large, N_BUCKETS - 1)
    return np.where(n < max_exact, n, large).astype(np.int32)


def _silu(x):
    return x * jax.nn.sigmoid(x)


def _layer_norm_rows(x, g, b):
    mu = jnp.mean(x, axis=-1, keepdims=True)
    xc = x - mu
    var = jnp.mean(xc * xc, axis=-1, keepdims=True)
    return xc * lax.rsqrt(var + LN_EPS) * g + b


def _rms_norm_rows(x, g):
    return x * lax.rsqrt(jnp.mean(x * x, axis=-1, keepdims=True) + LN_EPS) * g


def _bias_kernel(tab_ref, idx_ref, o_ref):
    h = pl.program_id(0)
    idx = idx_ref[0]
    far = tab_ref[N_BUCKETS - 1, h]
    acc = jnp.full(idx.shape, NEG, _F32)
    for b in range(N_BUCKETS):
        acc = jnp.where(idx == b, (tab_ref[b, h] - far) * LOG2E, acc)
    o_ref[0, 0] = acc


def _bias_tiles(table, idx):
    n, r, c = idx.shape
    return pl.pallas_call(
        _bias_kernel,
        grid=(H_A, n),
        in_specs=[pl.BlockSpec(memory_space=pltpu.SMEM),
                  pl.BlockSpec((1, r, c), lambda h, i: (i, 0, 0))],
        out_specs=pl.BlockSpec((1, 1, r, c), lambda h, i: (h, i, 0, 0)),
        out_shape=jax.ShapeDtypeStruct((H_A, n, r, c), _F32),
        name="bias_tiles",
    )(table, idx)


def _prompt_proj_kernel(x_ref, w_ref, lng_ref, lnb_ref, wt_ref, bs_ref,
                        kf_ref, vf_ref, kb_ref, qt_ref, vt_ref, ga_ref, g_ref):
    xb = x_ref[0].astype(_BF16)
    hm = jnp.dot(xb, w_ref[...], preferred_element_type=_F32)
    k = hm[:, W_A:2 * W_A]
    v = hm[:, 2 * W_A:3 * W_A]
    for h in range(H_A):
        cols = slice(h * DK, (h + 1) * DK)
        kf_ref[0, pl.ds(h, TM, stride=H_A), :] = k[:, cols]
        vf_ref[0, pl.ds(h, TM, stride=H_A), :] = v[:, cols]
    kb_ref[0] = k.astype(_BF16)
    ga_ref[0] = hm[:, 3 * W_A:4 * W_A]
    qt_ref[0] = hm[:, 0:W_A].T.astype(_BF16)
    vt = v.T.astype(_BF16)
    for h in range(H_A):
        vt_ref[0, h, 0:DK, :] = vt[h * DK:(h + 1) * DK]
        vt_ref[0, h, DK:VA, :] = jnp.ones((ONES_ROWS, TM), _BF16)
    o = 4 * W_A
    u = hm[:, o:o + W_B]
    vb = hm[:, o + W_B:o + 2 * W_B]
    gb = hm[:, o + 2 * W_B:o + 3 * W_B]
    vn = _layer_norm_rows(vb, lng_ref[...], lnb_ref[...]).astype(_BF16)
    gate = u * _silu(gb)
    n_chunks = TM // CHUNK
    for h in range(H_B):
        cols = slice(h * CB, (h + 1) * CB)
        vn_h = jnp.concatenate([vn[c * CHUNK:(c + 1) * CHUNK, cols] for c in range(n_chunks)], axis=1)
        s = jnp.dot(wt_ref[h], vn_h, preferred_element_type=_F32)
        for c in range(n_chunks):
            rows = slice(c * CHUNK, (c + 1) * CHUNK)
            g_ref[0, rows, cols] = (gate[rows, cols] * (s[:, c * CB:(c + 1) * CB] + bs_ref[h])).astype(_BF16)


def _prompt_proj(x, w_all, ln_g, ln_b, wt, bsb):
    b, s, d = x.shape
    row = lambda bi, si: (bi, si, 0)
    col = lambda bi, si: (bi, 0, si)
    const2 = lambda bi, si: (0, 0)
    const3 = lambda bi, si: (0, 0, 0)
    f32_heads = jax.ShapeDtypeStruct((b, s * H_A, DK), _F32)
    bf_rows = jax.ShapeDtypeStruct((b, s, W_A), _BF16)
    bf_cols = jax.ShapeDtypeStruct((b, W_A, s), _BF16)
    return pl.pallas_call(
        _prompt_proj_kernel,
        grid=(b, s // TM),
        in_specs=[pl.BlockSpec((1, TM, d), row),
                  pl.BlockSpec(w_all.shape, const2),
                  pl.BlockSpec(ln_g.shape, const2),
                  pl.BlockSpec(ln_b.shape, const2),
                  pl.BlockSpec(wt.shape, const3),
                  pl.BlockSpec(bsb.shape, const3)],
        out_specs=[pl.BlockSpec((1, TM * H_A, DK), row), pl.BlockSpec((1, TM * H_A, DK), row),
                   pl.BlockSpec((1, TM, W_A), row),
                   pl.BlockSpec((1, W_A, TM), col),
                   pl.BlockSpec((1, H_A, VA, TM), lambda bi, si: (bi, 0, 0, si)),
                   pl.BlockSpec((1, TM, W_A), row), pl.BlockSpec((1, TM, W_B), row)],
        out_shape=[f32_heads, f32_heads, bf_rows, bf_cols,
                   jax.ShapeDtypeStruct((b, H_A, VA, s), _BF16),
                   jax.ShapeDtypeStruct((b, s, W_A), _F32), jax.ShapeDtypeStruct((b, s, W_B), _BF16)],
        compiler_params=pltpu.CompilerParams(dimension_semantics=("arbitrary", "arbitrary"),
                                             vmem_limit_bytes=VMEM_LIMIT),
        name="prompt_proj",
    )(x, w_all, ln_g, ln_b, wt, bsb)


def _prompt_attn_kernel(*refs, lam_init, n_q):
    def body(qi, carry):
        _prompt_attn_block(qi, n_q, *refs, lam_init=lam_init)
        return carry

    lax.fori_loop(0, n_q, body, 0)


def _prompt_attn_block(qi, n_q, lam_ref, qt_ref, k_ref, vt_ref, bias_ref, ga_ref, sg_ref, a_ref,
                       qz_ref, qzn_ref, sa_ref, sb_ref, sc_ref, mxa_ref, mxb_ref, mxc_ref, m_ref, l_ref, acc_ref,
                       *, lam_init):
    q_rows = pl.ds(pl.multiple_of(qi * TQ, TQ), TQ)

    def split_maps(qt, dst_ref):
        row = lax.broadcasted_iota(jnp.int32, qt.shape, 0)
        zero = jnp.zeros_like(qt)
        dst_ref[:, 0:TQ] = jnp.where(row < HD, qt, zero)
        dst_ref[:, TQ:2 * TQ] = jnp.where(row >= HD, qt, zero)

    def begin():
        split_maps(qt_ref[0, :, q_rows], qz_ref)
        m_ref[...] = jnp.full(m_ref.shape, NEG, _F32)
        l_ref[...] = jnp.zeros(l_ref.shape, _F32)
        acc_ref[...] = jnp.zeros(acc_ref.shape, _F32)

    def finish():
        lam = lam_ref[0, 0]
        o_t = acc_ref[0] * (1.0 / l_ref[0]) - lam * (acc_ref[1] * (1.0 / l_ref[1]))
        o = o_t.T
        o = _rms_norm_rows(o, sg_ref[...]) * (1.0 - lam_init)
        a_ref[0, q_rows, :] = (o * _silu(ga_ref[0, q_rows, :])).astype(_BF16)

    def scores(j, dst_ref, mx_ref, bias_idx=None, q_ref=qz_ref):
        start = pl.multiple_of(j * TK, TK)
        s = jnp.dot(k_ref[0, pl.ds(start, TK), :], q_ref[...], preferred_element_type=_F32)
        if bias_idx == 1:
            bias = bias_ref[0, 1]
            s = jnp.concatenate([s[:, 0:TQ] + bias, s[:, TQ:2 * TQ] + bias], axis=1)
        elif bias_idx == 0:
            md = MAX_DISTANCE
            corner = bias_ref[0, 0, TK - md:TK, 0:md]
            low = s[TK - md:TK]
            low = jnp.concatenate([low[:, 0:md] + corner, low[:, md:TQ],
                                   low[:, TQ:TQ + md] + corner, low[:, TQ + md:2 * TQ]], axis=1)
            s = jnp.concatenate([s[0:TK - md], low], axis=0)
        dst_ref[...] = s
        mx_ref[...] = jnp.max(s, axis=0, keepdims=True)

    def consume(src_ref, mx_ref, j, diagonal=False):
        start = pl.multiple_of(j * TK, TK)
        vblk = vt_ref[0, 0, :, pl.ds(start, TK)]
        for mp in range(2):
            for hf in range(TQ // QH):
                qs = slice(hf * QH, (hf + 1) * QH)
                cols = slice(mp * TQ + hf * QH, mp * TQ + (hf + 1) * QH)
                n_keys = (hf + 1) * QH if diagonal else TK
                m_old = m_ref[mp, :, qs]
                m_new = jnp.maximum(m_old, mx_ref[:, cols])
                alpha = jnp.exp2(m_old - m_new)
                p = jnp.exp2(src_ref[0:n_keys, cols] - m_new).astype(_BF16)
                pv = jnp.dot(vblk[:, 0:n_keys], p, preferred_element_type=_F32)
                acc_ref[mp, :, qs] = acc_ref[mp, :, qs] * alpha + pv[0:DK]
                l_ref[mp, :, qs] = l_ref[mp, :, qs] * alpha + pv[DK:DK + 1]
                m_ref[mp, :, qs] = m_new

    def next_diagonal():
        jn = jnp.minimum(qi + 1, n_q - 1)
        split_maps(qt_ref[0, :, pl.ds(pl.multiple_of(jn * TQ, TQ), TQ)], qzn_ref)
        scores(jn, sc_ref, mxc_ref, bias_idx=1, q_ref=qzn_ref)

    n_far = jnp.maximum(qi - 1, 0)
    n_rest = jnp.maximum(n_far - 1, 0)
    odd_rest = jnp.bitwise_and(n_rest, 1)

    @pl.when(qi == 0)
    def _():
        begin()
        scores(0, sc_ref, mxc_ref, bias_idx=1)
        consume(sc_ref, mxc_ref, 0, diagonal=True)
        finish()

    @pl.when(qi == 1)
    def _():
        begin()
        scores(1, sc_ref, mxc_ref, bias_idx=1)
        scores(0, sb_ref, mxb_ref, bias_idx=0)
        consume(sc_ref, mxc_ref, 1, diagonal=True)
        consume(sb_ref, mxb_ref, 0)
        next_diagonal()
        finish()

    @pl.when(qi >= 2)
    def _():
        begin()
        scores(qi - 1, sb_ref, mxb_ref, bias_idx=0)
        consume(sc_ref, mxc_ref, qi, diagonal=True)
        scores(0, sa_ref, mxa_ref)
        consume(sb_ref, mxb_ref, qi - 1)

    def far_tiles(f0, count):
        for t in range(0, count, 2):
            scores(f0 + t + 1, sb_ref, mxb_ref)
            consume(sa_ref, mxa_ref, f0 + t)
            scores(f0 + t + 2, sa_ref, mxa_ref)
            consume(sb_ref, mxb_ref, f0 + t + 1)

    def far_quad(i, carry):
        far_tiles(4 * i, 4)
        return carry

    n_quads = lax.shift_right_logical(n_rest, 2)
    lax.fori_loop(0, n_quads, far_quad, 0)

    @pl.when(jnp.bitwise_and(n_rest, 2) == 2)
    def _():
        far_tiles(4 * n_quads, 2)

    @pl.when(jnp.logical_and(qi >= 2, odd_rest == 0))
    def _():
        consume(sa_ref, mxa_ref, n_far - 1)
        next_diagonal()
        finish()

    @pl.when(jnp.logical_and(qi >= 2, odd_rest == 1))
    def _():
        scores(n_far - 1, sb_ref, mxb_ref)
        consume(sa_ref, mxa_ref, n_far - 2)
        consume(sb_ref, mxb_ref, n_far - 1)
        next_diagonal()
        finish()


def _prompt_attn(lam, qt, kb, vt, bias, ga, subln_g, lam_init):
    b, s, _ = kb.shape
    return pl.pallas_call(
        functools.partial(_prompt_attn_kernel, lam_init=lam_init, n_q=s // TQ),
        grid=(b, H_A),
        in_specs=[pl.BlockSpec(memory_space=pltpu.SMEM),
                  pl.BlockSpec((1, DK, s), lambda bi, h: (bi, h, 0)),
                  pl.BlockSpec((1, s, DK), lambda bi, h: (bi, 0, h)),
                  pl.BlockSpec((1, 1, VA, s), lambda bi, h: (bi, h, 0, 0)),
                  pl.BlockSpec((1, 2, TK, TQ), lambda bi, h: (h, 0, 0, 0)),
                  pl.BlockSpec((1, s, DK), lambda bi, h: (bi, 0, h)),
                  pl.BlockSpec(subln_g.shape, lambda bi, h: (0, 0))],
        out_specs=pl.BlockSpec((1, s, DK), lambda bi, h: (bi, 0, h)),
        out_shape=jax.ShapeDtypeStruct((b, s, W_A), _BF16),
        scratch_shapes=[pltpu.VMEM((DK, 2 * TQ), _BF16),
                        pltpu.VMEM((DK, 2 * TQ), _BF16),
                        pltpu.VMEM((TK, 2 * TQ), _F32),
                        pltpu.VMEM((TK, 2 * TQ), _F32),
                        pltpu.VMEM((TK, 2 * TQ), _F32),
                        pltpu.VMEM((1, 2 * TQ), _F32),
                        pltpu.VMEM((1, 2 * TQ), _F32),
                        pltpu.VMEM((1, 2 * TQ), _F32),
                        pltpu.VMEM((2, 1, TQ), _F32),
                        pltpu.VMEM((2, 1, TQ), _F32),
                        pltpu.VMEM((2, DK, TQ), _F32)],
        compiler_params=pltpu.CompilerParams(dimension_semantics=("arbitrary", "arbitrary"),
                                             vmem_limit_bytes=VMEM_LIMIT),
        name="prompt_attn",
    )(lam, qt, kb, vt, bias, ga, subln_g)


def _out_proj_kernel(a_ref, g_ref, x_ref, wa_ref, wg_ref, lng_ref, lnb_ref, y_ref, *, alpha):
    sub = min(OUT_SUB, a_ref.shape[0])
    for i in range(a_ref.shape[0] // sub):
        rows = slice(i * sub, (i + 1) * sub)
        out = (jnp.dot(a_ref[rows, :], wa_ref[...], preferred_element_type=_F32)
               + jnp.dot(g_ref[rows, :], wg_ref[...], preferred_element_type=_F32))
        z = alpha * x_ref[rows, :] + out
        y_ref[rows, :] = _layer_norm_rows(z, lng_ref[...], lnb_ref[...])


def _out_proj(a, g, x, wo_a, wo_g, ln_g, ln_b, alpha):
    n, d = x.shape
    tm = min(n, TM_OUT)
    assert n % tm == 0
    row = lambda i: (i, 0)
    const = lambda i: (0, 0)
    return pl.pallas_call(
        functools.partial(_out_proj_kernel, alpha=alpha),
        grid=(n // tm,),
        in_specs=[pl.BlockSpec((tm, W_A), row), pl.BlockSpec((tm, W_B), row), pl.BlockSpec((tm, d), row),
                  pl.BlockSpec(wo_a.shape, const), pl.BlockSpec(wo_g.shape, const),
                  pl.BlockSpec(ln_g.shape, const), pl.BlockSpec(ln_b.shape, const)],
        out_specs=pl.BlockSpec((tm, d), row),
        out_shape=jax.ShapeDtypeStruct((n, d), _F32),
        compiler_params=pltpu.CompilerParams(dimension_semantics=("arbitrary",),
                                             vmem_limit_bytes=VMEM_LIMIT),
        name="out_proj",
    )(a, g, x, wo_a, wo_g, ln_g, ln_b)


def _sample_proj_kernel(x_ref, w_ref, lng_ref, lnb_ref, cs_ref, bs_ref, lp_ref,
                        q_ref, k_ref, v_ref, ga_ref, g_ref, vn_ref, lam_ref, *, lam_init, t_valid):
    rows = x_ref.shape[0]
    h = jnp.dot(x_ref[...].astype(_BF16), w_ref[...], preferred_element_type=_F32)
    q_ref[...] = h[:, 0:W_A]
    k_ref[...] = h[:, W_A:2 * W_A]
    v_ref[...] = h[:, 2 * W_A:3 * W_A]
    ga_ref[...] = h[:, 3 * W_A:4 * W_A]
    o = 4 * W_A
    u = h[:, o:o + W_B]
    vb = h[:, o + W_B:o + 2 * W_B]
    gb = h[:, o + 2 * W_B:o + 3 * W_B]
    vn = _layer_norm_rows(vb, lng_ref[...], lnb_ref[...])
    vn_ref[...] = vn
    vn3 = vn.reshape(rows // T_PAD, T_PAD, W_B)
    s3 = jnp.broadcast_to(bs_ref[...][None], vn3.shape)
    for j in range(t_valid):
        s3 = s3 + vn3[:, j:j + 1, :] * cs_ref[j][None]
    g_ref[...] = (u * s3.reshape(rows, W_B) * _silu(gb)).astype(_BF16)
    lp = lp_ref[...]
    e1 = jnp.exp(jnp.sum(lp[0:1] * lp[1:2], axis=-1, keepdims=True))
    e2 = jnp.exp(jnp.sum(lp[2:3] * lp[3:4], axis=-1, keepdims=True))
    lam_ref[...] = jnp.broadcast_to(e1 - e2 + lam_init, lam_ref.shape)


def _sample_proj(x, w, ln_g, ln_b, cs, bs8, lam_params, lam_init, t_valid):
    n, d = x.shape
    tm = min(n, 256)
    row = lambda i: (i, 0)
    const2 = lambda i: (0, 0)
    const3 = lambda i: (0, 0, 0)
    f32_rows = jax.ShapeDtypeStruct((n, W_A), _F32)
    return pl.pallas_call(
        functools.partial(_sample_proj_kernel, lam_init=lam_init, t_valid=t_valid),
        grid=(n // tm,),
        in_specs=[pl.BlockSpec((tm, d), row), pl.BlockSpec(w.shape, const2),
                  pl.BlockSpec(ln_g.shape, const2), pl.BlockSpec(ln_b.shape, const2),
                  pl.BlockSpec(cs.shape, const3), pl.BlockSpec(bs8.shape, const2),
                  pl.BlockSpec(lam_params.shape, const2)],
        out_specs=[pl.BlockSpec((tm, W_A), row)] * 4
                  + [pl.BlockSpec((tm, W_B), row), pl.BlockSpec((tm, W_B), row),
                     pl.BlockSpec((8, 128), const2)],
        out_shape=[f32_rows] * 4 + [jax.ShapeDtypeStruct((n, W_B), _BF16),
                                    jax.ShapeDtypeStruct((n, W_B), _F32),
                                    jax.ShapeDtypeStruct((8, 128), _F32)],
        compiler_params=pltpu.CompilerParams(dimension_semantics=("arbitrary",),
                                             vmem_limit_bytes=VMEM_LIMIT),
        name="sample_proj",
    )(x, w, ln_g, ln_b, cs, bs8, lam_params)


def _sample_attn_kernel(pt_ref, lam_ref, q_ref, kn_ref, vn_ref, ga_ref, hm_ref, neg_ref, bl_ref, bn_ref, sg_ref,
                        ck_ref, cv_ref, a_ref, kbuf, vbuf, ksem, vsem, qm_ref, s0_ref, s1_ref, m_ref, l_ref, acc_ref,
                        *, lam_init, n_groups, n_seq):
    gp = PAGES_PER_GROUP
    b = pl.program_id(0)
    n_rows = 2 * H_A * T_PAD
    page_rows = kbuf.shape[1]

    def group_copies(hbm_ref, buf, sem, seq, grp):
        slot = jnp.bitwise_and(grp, RING_GROUPS - 1)
        return [pltpu.make_async_copy(hbm_ref.at[pt_ref[seq, grp * gp + i]], buf.at[slot * gp + i], sem.at[slot])
                for i in range(gp)]

    def start_ahead(hbm_ref, buf, sem, grp):
        ahead = jnp.asarray(grp + RING_GROUPS, jnp.int32)
        wraps = (ahead >= n_groups).astype(jnp.int32)
        seq = b + wraps
        grp2 = ahead - wraps * n_groups

        @pl.when(seq < n_seq)
        def _():
            for i, c in enumerate(group_copies(hbm_ref, buf, sem, seq, grp2)):
                c.start(priority=i % 2)

    def wait_group(hbm_ref, buf, sem, grp):
        for c in group_copies(hbm_ref, buf, sem, b, grp):
            c.wait()

    @pl.when(b == 0)
    def _():
        for g in range(RING_GROUPS):
            for i, c in enumerate(group_copies(ck_ref, kbuf, ksem, 0, g) + group_copies(cv_ref, vbuf, vsem, 0, g)):
                c.start(priority=i % 2)

    q8 = q_ref[0]
    per_head = [q8[:, h * DK:(h + 1) * DK] for h in range(H_A)]
    qm_ref[...] = jnp.concatenate(per_head + per_head, axis=0) * hm_ref[...]
    m_ref[...] = jnp.full(m_ref.shape, NEG, _F32)
    l_ref[...] = jnp.zeros(l_ref.shape, _F32)
    acc_ref[...] = jnp.zeros(acc_ref.shape, _F32)

    def scores(grp, dst_ref, with_last_bias):
        slot = jnp.bitwise_and(grp, RING_GROUPS - 1)
        qm = qm_ref[...]
        for i in range(gp):
            s = lax.dot_general(qm, kbuf[slot * gp + i], _NT, preferred_element_type=_F32) + neg_ref[...]
            if with_last_bias and i == gp - 1:
                s = s + bl_ref[...]
            dst_ref[:, i * page_rows:(i + 1) * page_rows] = s

    def consume(grp, src_ref):
        slot = jnp.bitwise_and(grp, RING_GROUPS - 1)
        smax = src_ref[:, 0:page_rows]
        for i in range(1, gp):
            smax = jnp.maximum(smax, src_ref[:, i * page_rows:(i + 1) * page_rows])
        m_old = m_ref[...]
        m_new = jnp.maximum(m_old, jnp.max(smax, axis=-1, keepdims=True))
        alpha = jnp.exp2(m_old - m_new)
        l = alpha * l_ref[...]
        acc = alpha * acc_ref[...]
        for i in range(gp):
            p = jnp.exp2(src_ref[:, i * page_rows:(i + 1) * page_rows] - m_new)
            l = l + p
            acc = acc + jnp.dot(p, vbuf[slot * gp + i], preferred_element_type=_F32)
        l_ref[...] = l
        acc_ref[...] = acc
        m_ref[...] = m_new

    def block(grp, dst_ref, src_ref, with_last_bias=False):
        wait_group(ck_ref, kbuf, ksem, grp)
        wait_group(cv_ref, vbuf, vsem, grp - 1)
        scores(grp, dst_ref, with_last_bias)
        consume(grp - 1, src_ref)
        start_ahead(ck_ref, kbuf, ksem, grp)
        start_ahead(cv_ref, vbuf, vsem, grp - 1)

    wait_group(ck_ref, kbuf, ksem, 0)
    scores(0, s0_ref, False)
    start_ahead(ck_ref, kbuf, ksem, 0)

    def pair(i, carry):
        g = 2 * i + 1
        block(g, s1_ref, s0_ref)
        block(g + 1, s0_ref, s1_ref)
        return carry

    lax.fori_loop(0, (n_groups - 2) // 2, pair, 0)
    block(n_groups - 1, s1_ref, s0_ref, with_last_bias=True)
    wait_group(cv_ref, vbuf, vsem, n_groups - 1)
    consume(n_groups - 1, s1_ref)
    start_ahead(cv_ref, vbuf, vsem, n_groups - 1)

    sn = lax.dot_general(qm_ref[...], kn_ref[0], _NT, preferred_element_type=_F32) + bn_ref[...]
    m_o = m_ref[...]
    m_f = jnp.maximum(m_o, jnp.max(sn, axis=-1, keepdims=True))
    al = jnp.exp2(m_o - m_f)
    pn = jnp.exp2(sn - m_f)
    l_tot = al * jnp.sum(l_ref[...], axis=-1, keepdims=True) + jnp.sum(pn, axis=-1, keepdims=True)
    acc = al * acc_ref[...] + jnp.dot(pn, vn_ref[0], preferred_element_type=_F32)
    o_full = acc * (1.0 / l_tot)
    half = n_rows // 2
    od = o_full[0:half] - lam_ref[0, 0] * o_full[half:n_rows]
    ga = ga_ref[0]
    for h in range(H_A):
        cols = slice(h * DK, (h + 1) * DK)
        oh = _rms_norm_rows(od[h * T_PAD:(h + 1) * T_PAD], sg_ref[...]) * (1.0 - lam_init)
        a_ref[0, :, cols] = (oh * _silu(ga[:, cols])).astype(_BF16)


def _sample_attn(page_table, lam, q8, kn, vn, ga8, halfmask, negmask, bias_last, bias_new, subln_g, ck, cv,
                 lam_init):
    db, n_pages = page_table.shape
    gp = PAGES_PER_GROUP
    n_groups = n_pages // gp
    assert n_pages % gp == 0 and n_groups % RING_GROUPS == 0 and n_groups % 2 == 0
    page_rows = ck.shape[1]
    n_rows = 2 * H_A * T_PAD
    seq = lambda bi, pt: (bi, 0, 0)
    const = lambda bi, pt: (0, 0)
    grid_spec = pltpu.PrefetchScalarGridSpec(
        num_scalar_prefetch=1,
        grid=(db,),
        in_specs=[pl.BlockSpec(memory_space=pltpu.SMEM),
                  pl.BlockSpec((1, T_PAD, W_A), seq),
                  pl.BlockSpec((1, T_PAD * H_A, DK), seq), pl.BlockSpec((1, T_PAD * H_A, DK), seq),
                  pl.BlockSpec((1, T_PAD, W_A), seq),
                  pl.BlockSpec(halfmask.shape, const), pl.BlockSpec(negmask.shape, const),
                  pl.BlockSpec(bias_last.shape, const), pl.BlockSpec(bias_new.shape, const),
                  pl.BlockSpec(subln_g.shape, const),
                  pl.BlockSpec(memory_space=pl.ANY), pl.BlockSpec(memory_space=pl.ANY)],
        out_specs=pl.BlockSpec((1, T_PAD, W_A), seq),
        scratch_shapes=[pltpu.VMEM((RING_GROUPS * gp, page_rows, DK), _F32),
                        pltpu.VMEM((RING_GROUPS * gp, page_rows, DK), _F32),
                        pltpu.SemaphoreType.DMA((RING_GROUPS,)),
                        pltpu.SemaphoreType.DMA((RING_GROUPS,)),
                        pltpu.VMEM((n_rows, DK), _F32),
                        pltpu.VMEM((n_rows, gp * page_rows), _F32),
                        pltpu.VMEM((n_rows, gp * page_rows), _F32),
                        pltpu.VMEM((n_rows, 1), _F32),
                        pltpu.VMEM((n_rows, page_rows), _F32),
                        pltpu.VMEM((n_rows, DK), _F32)],
    )
    return pl.pallas_call(
        functools.partial(_sample_attn_kernel, lam_init=lam_init, n_groups=n_groups, n_seq=db),
        grid_spec=grid_spec,
        out_shape=jax.ShapeDtypeStruct((db, T_PAD, W_A), _BF16),
        compiler_params=pltpu.CompilerParams(dimension_semantics=("arbitrary",),
                                             vmem_limit_bytes=VMEM_LIMIT),
        name="sample_attn",
    )(page_table, lam, q8, kn, vn, ga8, halfmask, negmask, bias_last, bias_new, subln_g, ck, cv)


def kernel(x_prompt, x_sample, cache_k, cache_v, page_table, w_in, w_out, lambda_q1, lambda_k1, lambda_q2,
           lambda_k2, subln_g, rel_bias, sgu_ln_g, sgu_ln_b, sgu_w, sgu_b, post_ln_g, post_ln_b):
    depth = w_in.shape[0]
    assert depth == 1, "single trunk layer"
    b, s, d = x_prompt.shape
    db, t, _ = x_sample.shape
    n_pool, page = cache_k.shape[1], cache_k.shape[2]
    assert s % TQ == 0 and s % TM == 0 and TQ == TK and TQ >= MAX_DISTANCE
    assert t <= T_PAD and page >= MAX_DISTANCE
    lam_init = 0.8 - 0.6 * math.exp(-0.3 * 0)
    alpha = (2.0 * depth) ** 0.25

    col_scale = jnp.where(jnp.arange(w_in.shape[2]) < W_A, SCALE * LOG2E, 1.0).astype(_F32)
    w_all = (w_in[0] * col_scale[None, :]).astype(_BF16)
    wo_a = w_out[0, 0:W_A].astype(_BF16)
    wo_g = w_out[0, W_A:].astype(_BF16)
    ln_g = sgu_ln_g[0][None]
    ln_b = sgu_ln_b[0][None]
    pg_ = post_ln_g[0][None]
    pb_ = post_ln_b[0][None]
    sg = subln_g[0][None]
    tril = jnp.tril(jnp.ones((CHUNK, CHUNK), _F32))
    wt = (sgu_w[0] * tril).astype(_BF16)
    bsb = jnp.broadcast_to(sgu_b[0][:, :, None], (H_B, CHUNK, CB))
    wsmall = sgu_w[0][:, :T_PAD, :T_PAD] * tril[:T_PAD, :T_PAD]
    valid_t = (jnp.arange(T_PAD) < t).astype(_F32)
    cs = jnp.transpose(wsmall, (2, 1, 0)) * valid_t[None, :, None]
    cs = jnp.repeat(cs, CB, axis=2)
    bs8 = jnp.repeat(jnp.transpose(sgu_b[0][:, :T_PAD]) * valid_t[:, None], CB, axis=1)
    lam_params = jnp.concatenate([lambda_q1, lambda_k1, lambda_q2, lambda_k2], axis=0)

    kk = np.arange(TK)[:, None]
    qq = np.arange(TQ)[None, :]
    idx_sub = _bucket_np(TQ + qq - kk)
    idx_diag = np.where(qq >= kk, _bucket_np(qq - kk), MASKED_BUCKET)
    bias_prompt = _bias_tiles(rel_bias, jnp.asarray(np.stack([idx_sub, idx_diag]).astype(np.int32)))
    tt = np.arange(T_PAD)[:, None]
    ii = np.arange(page * H_A)[None, :] // H_A
    idx_last = _bucket_np(np.where(tt < t, page + tt - ii, MAX_DISTANCE))
    idx_new = np.where((ii <= tt) & (tt < t), _bucket_np(tt - ii), MASKED_BUCKET)
    bias_sample = _bias_tiles(rel_bias, jnp.asarray(np.stack([idx_last, idx_new]).astype(np.int32)))
    r = np.arange(2 * H_A * T_PAD)
    r_head = (r // T_PAD) % H_A
    r_map = r // (H_A * T_PAD)
    lane_head = np.arange(page * H_A) % H_A
    own_head = r_head[:, None] == lane_head[None, :]
    negmask = jnp.asarray(np.where(own_head, 0.0, NEG).astype(np.float32))
    halfmask = jnp.asarray((np.arange(DK)[None, :] // HD == r_map[:, None]).astype(np.float32))
    bias_last = jnp.tile(bias_sample[:, 0].reshape(H_A * T_PAD, page * H_A), (2, 1))
    bias_new = (jnp.tile(bias_sample[:, 1, :, :T_PAD * H_A].reshape(H_A * T_PAD, T_PAD * H_A), (2, 1))
                + negmask[:, :T_PAD * H_A])

    xs = jnp.pad(x_sample, ((0, 0), (0, T_PAD - t), (0, 0))).reshape(db * T_PAD, d)
    q8, k8, v8, ga8, g8, vn8, lam_tile = _sample_proj(xs, w_all, ln_g, ln_b, cs, bs8, lam_params, lam_init, t)
    lam = lam_tile[0:1, 0:1]
    ck = cache_k[0].reshape(n_pool, page * H_A, DK)
    cv = cache_v[0].reshape(n_pool, page * H_A, DK)
    a8 = _sample_attn(page_table, lam, q8.reshape(db, T_PAD, W_A), k8.reshape(db, T_PAD * H_A, DK),
                      v8.reshape(db, T_PAD * H_A, DK), ga8.reshape(db, T_PAD, W_A), halfmask, negmask,
                      bias_last, bias_new, sg, ck, cv, lam_init)
    ys = _out_proj(a8.reshape(db * T_PAD, W_A), g8, xs, wo_a, wo_g, pg_, pb_, alpha)
    y_sample = ys.reshape(db, T_PAD, d)[:, :t]
    nks = k8.reshape(db, T_PAD, H_A, DK)[None, :, :t]
    nvs = v8.reshape(db, T_PAD, H_A, DK)[None, :, :t]
    nsv = vn8.reshape(db, T_PAD, H_B, CB)[None, :, :t]

    kf, vf, kb, qt, vt, ga, g = _prompt_proj(x_prompt, w_all, ln_g, ln_b, wt, bsb)
    a = _prompt_attn(lam, qt, kb, vt, bias_prompt, ga, sg, lam_init)
    yp = _out_proj(a.reshape(b * s, W_A), g.reshape(b * s, W_B), x_prompt.reshape(b * s, d),
                   wo_a, wo_g, pg_, pb_, alpha)
    y_prompt = yp.reshape(b, s, d)
    nkp = kf.reshape(1, b, s, H_A, DK)
    nvp = vf.reshape(1, b, s, H_A, DK)
    return (y_prompt, y_sample, nkp, nvp, nks, nvs, nsv)
```
